```python
import math
import jax, jax.numpy as jnp
from jax import lax
import numpy as np

D_MODEL = 1024
BATCH = 2
SEQ = 16384
DEPTH = 2

D_FF = 2816
PLE_DIM = 256
EPS = 1e-6
Q_BLOCK = 128
A_HEADS = 16
A_DC = 128
A_DCQ = 256
IDX_HEADS = 8
IDX_DIM = 64
TOPK_MAX = 256
SSM_DINNER = 2 * D_MODEL
SSM_HEADDIM = 64
SSM_HEADS = SSM_DINNER // SSM_HEADDIM
SSM_GROUPS = 8
SSM_STATE = 128
SSM_CONV = 4
SSM_CHUNK = 128
SSM_CONV_DIM = SSM_DINNER + 2 * SSM_GROUPS * SSM_STATE
C_GROUPS = ((128, 1), (512, 4), (2048, 16))
C_HEADS = 8
C_HEADDIM = 64
C_COLS = len(C_GROUPS) * 3 * C_HEADS * C_HEADDIM
REL_BUCKETS = 32
REL_MAX_DIST = 2048
REL_HEADS = A_HEADS + len(C_GROUPS) * C_HEADS
IN_SPLIT_SIZES = (A_DCQ, A_DC, IDX_DIM, IDX_HEADS,
                  SSM_DINNER, SSM_CONV_DIM, SSM_HEADS,
                  C_COLS,
                  D_MODEL, D_MODEL, D_MODEL)
IN_COLS = sum(IN_SPLIT_SIZES)

kernel_name = 'hybrid_gated_dsa_ssd_dilated'


def rmsnorm(x, g):
    xf = x.astype(jnp.float32)
    y = xf * lax.rsqrt(jnp.mean(xf * xf, axis=-1, keepdims=True) + EPS)
    return (y * g.astype(jnp.float32)).astype(x.dtype)


def swiglu(h, w_gate, w_up, w_down):
    return (jax.nn.silu(h @ w_gate) * (h @ w_up)) @ w_down


def to_blocks(a, size=Q_BLOCK):
    a = a.reshape(a.shape[0], a.shape[1] // size, size, *a.shape[2:])
    return jnp.swapaxes(a, 0, 1)


def from_blocks(a):
    a = jnp.swapaxes(a, 0, 1)
    return a.reshape(a.shape[0], a.shape[1] * a.shape[2], *a.shape[3:])


def rel_bucket(dist):
    max_exact = REL_BUCKETS // 2
    d = jnp.maximum(dist, 0)
    df = jnp.maximum(d, 1).astype(jnp.float32)
    large = max_exact + (jnp.log(df / max_exact) / math.log(REL_MAX_DIST / max_exact)
                         * (REL_BUCKETS - max_exact)).astype(jnp.int32)
    large = jnp.minimum(large, REL_BUCKETS - 1)
    return jnp.where(d < max_exact, d, large)


def dsa_mixer(cq, ckv, k_idx, w_idx, cq_norm, ckv_norm, w_uq, w_iq, bias_tbl):
    bsz, t, _ = cq.shape
    topk = min(TOPK_MAX, t // 4)
    cqn = rmsnorm(cq, cq_norm)
    q_lat = (cqn @ w_uq).reshape(bsz, t, A_HEADS, A_DC)
    q_idx = (cqn @ w_iq).reshape(bsz, t, IDX_HEADS, IDX_DIM)
    kv = rmsnorm(ckv, ckv_norm)
    w_idx = w_idx * (IDX_HEADS * IDX_DIM) ** -0.5
    key_pos = jnp.arange(t)
    tbl = bias_tbl[:, :A_HEADS]
    scale = A_DC ** -0.5

    def block(args):
        qb, qib, wb, pos = args
        rel = jax.nn.relu(jnp.einsum('bqhd,bsd->bqhs', qib, k_idx))
        score = jnp.einsum('bqhs,bqh->bqs', rel, wb).astype(jnp.float32)
        causal = key_pos[None, :] <= pos[:, None]
        score = jnp.where(causal[None], score, -jnp.inf)
        _, sel = lax.top_k(score, topk)
        kv_sel = jax.vmap(lambda c, i: c[i])(kv, sel)
        dist = pos[None, :, None] - sel
        bias = jnp.swapaxes(tbl[rel_bucket(dist)], -1, -2).astype(jnp.float32)
        logits = jnp.einsum('bqhd,bqkd->bqhk', qb, kv_sel).astype(jnp.float32) * scale + bias
        logits = jnp.where((dist >= 0)[:, :, None, :], logits, -jnp.inf)
        probs = jax.nn.softmax(logits, axis=-1).astype(kv_sel.dtype)
        return jnp.einsum('bqhk,bqkd->bqhd', probs, kv_sel)

    out = lax.map(block, (to_blocks(q_lat), to_blocks(q_idx), to_blocks(w_idx),
                          key_pos.reshape(-1, Q_BLOCK)))
    return from_blocks(out).reshape(bsz, t, A_HEADS * A_DC)


def ssd_mixer(z, xbc, dt_raw, conv_w, conv_b, dt_bias, a_log, d_skip, norm_g):
    bsz, t, _ = xbc.shape
    G, HG, P, N = SSM_GROUPS, SSM_HEADS // SSM_GROUPS, SSM_HEADDIM, SSM_STATE
    conv = lax.conv_general_dilated(xbc, conv_w, window_strides=(1,),
                                    padding=[(SSM_CONV - 1, 0)],
                                    dimension_numbers=('NWC', 'WIO', 'NWC'),
                                    feature_group_count=SSM_CONV_DIM)
    xbc = jax.nn.silu(conv + conv_b)
    xs, bm, cm = jnp.split(xbc, [SSM_DINNER, SSM_DINNER + G * N], axis=-1)
    xs = xs.reshape(bsz, t, G, HG, P).astype(jnp.float32)
    bm = bm.reshape(bsz, t, G, N).astype(jnp.float32)
    cm = cm.reshape(bsz, t, G, N).astype(jnp.float32)
    dt = jax.nn.softplus(dt_raw.astype(jnp.float32) + dt_bias.astype(jnp.float32)).reshape(bsz, t, G, HG)
    a = -jnp.exp(a_log.astype(jnp.float32)).reshape(G, HG)
    tril = jnp.tril(jnp.ones((SSM_CHUNK, SSM_CHUNK), dtype=bool))[None, :, :, None, None]

    def step(state, inp):
        xc, bc, cc, dtc = inp
        cum = jnp.cumsum(dtc * a, axis=1)
        seg = cum[:, :, None] - cum[:, None, :]
        decay = jnp.exp(jnp.where(tril, seg, -jnp.inf))
        cb = jnp.einsum('bign,bjgn->bijg', cc, bc)
        wgt = decay * cb[..., None] * dtc[:, None]
        y_diag = jnp.einsum('bijgh,bjghp->bighp', wgt, xc)
        y_off = jnp.einsum('bign,bghpn->bighp', cc, state) * jnp.exp(cum)[..., None]
        last = cum[:, -1]
        w_state = jnp.exp(last[:, None] - cum) * dtc
        new_state = (state * jnp.exp(last)[..., None, None]
                     + jnp.einsum('bjgh,bjghp,bjgn->bghpn', w_state, xc, bc))
        return new_state, y_diag + y_off

    state0 = jnp.zeros((bsz, G, HG, P, N), jnp.float32)
    _, ys = lax.scan(step, state0, (to_blocks(xs, SSM_CHUNK), to_blocks(bm, SSM_CHUNK),
                                    to_blocks(cm, SSM_CHUNK), to_blocks(dt, SSM_CHUNK)))
    y = from_blocks(ys) + xs * d_skip.astype(jnp.float32).reshape(G, HG)[:, :, None]
    y = y * jax.nn.silu(z.astype(jnp.float32)).reshape(bsz, t, G, HG, P)
    y = y.reshape(bsz, t, G, HG * P)
    y = y * lax.rsqrt(jnp.mean(y * y, axis=-1, keepdims=True) + EPS)
    y = y.reshape(bsz, t, SSM_DINNER) * norm_g.astype(jnp.float32)
    return y.astype(z.dtype)


def dilated_mixer(qkv, bias_tbl):
    bsz, t, _ = qkv.shape
    qkv = qkv.reshape(bsz, t, len(C_GROUPS), 3, C_HEADS, C_HEADDIM)
    q_pos = jnp.arange(t).reshape(-1, Q_BLOCK)
    scale = C_HEADDIM ** -0.5
    outs, lses = [], []
    for g, (window, dilation) in enumerate(C_GROUPS):
        q, k, v = qkv[:, :, g, 0], qkv[:, :, g, 1], qkv[:, :, g, 2]
        offs = jnp.arange(window // dilation + 1) * dilation
        tbl = bias_tbl[:, A_HEADS + g * C_HEADS:A_HEADS + (g + 1) * C_HEADS]
        bias = tbl[rel_bucket(offs)].T.astype(jnp.float32)

        def block(args, k=k, v=v, offs=offs, bias=bias):
            qb, pos = args
            idx = pos[:, None] - offs[None, :]
            valid = idx >= 0
            idx = jnp.maximum(idx, 0)
            kb = k[:, idx]
            vb = v[:, idx]
            logits = jnp.einsum('bqhd,bqkhd->bqhk', qb, kb).astype(jnp.float32) * scale + bias
            logits = jnp.where(valid[None, :, None, :], logits, -jnp.inf)
            lse = jax.nn.logsumexp(logits, axis=-1)
            probs = jnp.exp(logits - lse[..., None]).astype(vb.dtype)
            return jnp.einsum('bqhk,bqkhd->bqhd', probs, vb), lse

        o, lse = lax.map(block, (to_blocks(q), q_pos))
        outs.append(from_blocks(o))
        lses.append(from_blocks(lse))
    wts = jax.nn.softmax(jnp.stack(lses), axis=0)
    out = jnp.einsum('gbth,gbthd->bthd', wts.astype(qkv.dtype), jnp.stack(outs))
    return out.reshape(bsz, t, C_HEADS * C_HEADDIM)


def setup_inputs(seed: int = 0) -> dict:
    key = jax.random.key(seed)
    ks = iter(jax.random.split(key, 48))
    f32 = jnp.float32
    L = DEPTH

    def nrm(shape, fan_in):
        return jax.random.normal(next(ks), shape, f32) * fan_in ** -0.5

    def gain(shape):
        return 1.0 + 0.05 * jax.random.normal(next(ks), shape, f32)

    def small(shape, s=0.02):
        return s * jax.random.normal(next(ks), shape, f32)

    dt0 = jnp.exp(jax.random.uniform(next(ks), (L, SSM_HEADS), f32, math.log(1e-3), math.log(1e-1)))
    return {
        'x': jax.random.normal(next(ks), (BATCH, SEQ, D_MODEL), f32),
        'p': jax.random.normal(next(ks), (DEPTH, BATCH, SEQ, PLE_DIM), f32),
        'rel_bias': 0.5 * jax.random.normal(next(ks), (REL_BUCKETS, REL_HEADS), f32),
        'final_norm': gain((D_MODEL,)),
        'ffn1_norm': gain((L, D_MODEL)),
        'ffn1_gate': nrm((L, D_MODEL, D_FF), D_MODEL),
        'ffn1_up': nrm((L, D_MODEL, D_FF), D_MODEL),
        'ffn1_down': nrm((L, D_FF, D_MODEL), D_FF),
        'mix_norm': gain((L, D_MODEL)),
        'w_in': nrm((L, D_MODEL, IN_COLS), D_MODEL),
        'cq_norm': gain((L, A_DCQ)),
        'ckv_norm': gain((L, A_DC)),
        'w_uq': nrm((L, A_DCQ, A_HEADS * A_DC), A_DCQ),
        'w_iq': nrm((L, A_DCQ, IDX_HEADS * IDX_DIM), A_DCQ),
        'conv_w': nrm((L, SSM_CONV, 1, SSM_CONV_DIM), SSM_CONV),
        'conv_b': small((L, SSM_CONV_DIM)),
        'dt_bias': dt0 + jnp.log(-jnp.expm1(-dt0)),
        'a_log': jnp.log(jax.random.uniform(next(ks), (L, SSM_HEADS), f32, 1.0, 16.0)),
        'd_skip': gain((L, SSM_HEADS)),
        'ssm_norm': gain((L, SSM_DINNER)),
        'w_out_a': nrm((L, A_HEADS * A_DC, D_MODEL), A_HEADS * A_DC),
        'w_out_b': nrm((L, SSM_DINNER, D_MODEL), SSM_DINNER),
        'w_out_c': nrm((L, C_HEADS * C_HEADDIM, D_MODEL), C_HEADS * C_HEADDIM),
        'w_o': nrm((L, D_MODEL, D_MODEL), D_MODEL),
        'ffn2_norm': gain((L, D_MODEL)),
        'ffn2_gate': nrm((L, D_MODEL, D_FF), D_MODEL),
        'ffn2_up': nrm((L, D_MODEL, D_FF), D_MODEL),
        'ffn2_down': nrm((L, D_FF, D_MODEL), D_FF),
        'ple_norm': gain((L, D_MODEL)),
        'w_ple_gate': nrm((L, D_MODEL, D_MODEL), D_MODEL),
        'w_ple_proj': nrm((L, PLE_DIM, D_MODEL), PLE_DIM),
    }


def reference(x, p, rel_bias, final_norm,
              ffn1_norm, ffn1_gate, ffn1_up, ffn1_down,
              mix_norm, w_in, cq_norm, ckv_norm, w_uq, w_iq,
              conv_w, conv_b, dt_bias, a_log, d_skip, ssm_norm,
              w_out_a, w_out_b, w_out_c, w_o,
              ffn2_norm, ffn2_gate, ffn2_up, ffn2_down,
              ple_norm, w_ple_gate, w_ple_proj):
    split_at = np.cumsum(IN_SPLIT_SIZES)[:-1].tolist()
    for i in range(DEPTH):
        h = rmsnorm(x, ffn1_norm[i])
        x = x + 0.5 * swiglu(h, ffn1_gate[i], ffn1_up[i], ffn1_down[i])
        h = rmsnorm(x, mix_norm[i])
        (cq, ckv, k_idx, w_idx, z, xbc, dt_raw, qkv_c,
         g_a, g_b, g_c) = jnp.split(h @ w_in[i], split_at, axis=-1)
        y_a = dsa_mixer(cq, ckv, k_idx, w_idx, cq_norm[i], ckv_norm[i], w_uq[i], w_iq[i], rel_bias) @ w_out_a[i]
        y_b = ssd_mixer(z, xbc, dt_raw, conv_w[i], conv_b[i], dt_bias[i], a_log[i], d_skip[i], ssm_norm[i]) @ w_out_b[i]
        y_c = dilated_mixer(qkv_c, rel_bias) @ w_out_c[i]
        merged = jax.nn.sigmoid(g_a) * y_a + jax.nn.sigmoid(g_b) * y_b + jax.nn.sigmoid(g_c) * y_c
        x = x + merged @ w_o[i]
        h = rmsnorm(x, ffn2_norm[i])
        x = x + 0.5 * swiglu(h, ffn2_gate[i], ffn2_up[i], ffn2_down[i])
        h = rmsnorm(x, ple_norm[i])
        x = x + jax.nn.sigmoid(h @ w_ple_gate[i]) * (p[i] @ w_ple_proj[i])
    return rmsnorm(x, final_norm)
```

```python
import functools
import math

import jax
import jax.numpy as jnp
from jax import lax
from jax.experimental import pallas as pl
from jax.experimental.pallas import tpu as pltpu

F32 = jnp.float32
BF16 = jnp.bfloat16
I32 = jnp.int32

EPS = 1e-6
LANES = 128
Q_BLOCK = 128
VMEM_LIMIT = 56 * 1024 * 1024

A_HEADS = 16
A_DC = 128
A_DCQ = 256
IDX_HEADS = 8
IDX_DIM = 64
TOPK_MAX = 256
A_HEAD_GROUP = 4
A_KEY_TILE = 512
SSM_HEADDIM = 64
SSM_GROUPS = 8
SSM_STATE = 128
SSM_CONV = 4
SSM_CHUNK = 128
C_GROUPS = ((128, 1), (512, 4), (2048, 16))
C_HEADS = 8
C_HEADDIM = 64
REL_BUCKETS = 32
REL_MAX_DIST = 2048

NEG_BIG = -1e30
KEY_NEG_INF = -2139095041
INT_MIN = -2147483648


def _params(sem, vmem=VMEM_LIMIT):
    return pltpu.CompilerParams(dimension_semantics=sem, vmem_limit_bytes=vmem)


def _rms(x, g):
    return x * lax.rsqrt(jnp.mean(x * x, axis=-1, keepdims=True) + EPS) * g


def _sigmoid(x):
    return 1.0 / (1.0 + jnp.exp(-x))


def _rel_bucket(d):
    max_exact = REL_BUCKETS // 2
    d = jnp.maximum(d, 0)
    df = jnp.maximum(d, 1).astype(F32)
    large = max_exact + (jnp.log(df / max_exact) / math.log(REL_MAX_DIST / max_exact)
                         * (REL_BUCKETS - max_exact)).astype(I32)
    large = jnp.minimum(large, REL_BUCKETS - 1)
    return jnp.where(d < max_exact, d, large)


def _norm_matmul_kernel(x_ref, g_ref, w_ref, o_ref, hn_ref):
    @pl.when(pl.program_id(1) == 0)
    def _():
        hn_ref[...] = _rms(x_ref[...], g_ref[...]).astype(BF16)

    o_ref[...] = jnp.dot(hn_ref[...], w_ref[...], preferred_element_type=F32).astype(o_ref.dtype)


def _norm_matmul(x, g, w, out_dtype, tm, tn):
    m, k = x.shape
    n = w.shape[1]
    return pl.pallas_call(
        _norm_matmul_kernel,
        grid=(m // tm, n // tn),
        in_specs=[pl.BlockSpec((tm, k), lambda i, j: (i, 0)),
                  pl.BlockSpec((1, k), lambda i, j: (0, 0)),
                  pl.BlockSpec((k, tn), lambda i, j: (0, j))],
        out_specs=pl.BlockSpec((tm, tn), lambda i, j: (i, j)),
        out_shape=jax.ShapeDtypeStruct((m, n), out_dtype),
        scratch_shapes=[pltpu.VMEM((tm, k), BF16)],
        compiler_params=_params(("parallel", "arbitrary")),
        name="norm_matmul",
    )(x, g.reshape(1, k), w)


def _ffn_kernel(x_ref, g_ref, wg_ref, wu_ref, wd_ref, o_ref, hn_ref, acc_ref):
    j = pl.program_id(1)

    @pl.when(j == 0)
    def _():
        hn_ref[...] = _rms(x_ref[...], g_ref[...]).astype(BF16)
        acc_ref[...] = jnp.zeros_like(acc_ref)

    h = hn_ref[...]
    gate = jnp.dot(h, wg_ref[...], preferred_element_type=F32)
    up = jnp.dot(h, wu_ref[...], preferred_element_type=F32)
    act = (gate * _sigmoid(gate) * up).astype(BF16)
    acc_ref[...] += jnp.dot(act, wd_ref[...], preferred_element_type=F32)

    @pl.when(j == pl.num_programs(1) - 1)
    def _():
        o_ref[...] = x_ref[...] + 0.5 * acc_ref[...]


def _ffn(x, g, w_gate, w_up, w_down, tm=512):
    m, d = x.shape
    f = w_gate.shape[1]
    tf = f // 2 if (f // 2) % LANES == 0 else f
    return pl.pallas_call(
        _ffn_kernel,
        grid=(m // tm, f // tf),
        in_specs=[pl.BlockSpec((tm, d), lambda i, j: (i, 0)),
                  pl.BlockSpec((1, d), lambda i, j: (0, 0)),
                  pl.BlockSpec((d, tf), lambda i, j: (0, j)),
                  pl.BlockSpec((d, tf), lambda i, j: (0, j)),
                  pl.BlockSpec((tf, d), lambda i, j: (j, 0))],
        out_specs=pl.BlockSpec((tm, d), lambda i, j: (i, 0)),
        out_shape=jax.ShapeDtypeStruct((m, d), F32),
        scratch_shapes=[pltpu.VMEM((tm, d), BF16), pltpu.VMEM((tm, d), F32)],
        compiler_params=_params(("parallel", "arbitrary")),
        name="ffn",
    )(x, g.reshape(1, d), w_gate.astype(BF16), w_up.astype(BF16), w_down.astype(BF16))


def _ple_kernel(x_ref, p_ref, g_ref, wg_ref, wp_ref, fg_ref, o_ref, *, final_norm):
    x = x_ref[...]
    h = _rms(x, g_ref[...]).astype(BF16)
    gate = _sigmoid(jnp.dot(h, wg_ref[...], preferred_element_type=F32))
    proj = jnp.dot(p_ref[...].astype(BF16), wp_ref[...], preferred_element_type=F32)
    y = x + gate * proj
    if final_norm:
        y = _rms(y, fg_ref[...])
    o_ref[...] = y


def _ple(x, p, g, w_gate, w_proj, final_g, final_norm, tm=512):
    m, d = x.shape
    dp = p.shape[1]
    return pl.pallas_call(
        functools.partial(_ple_kernel, final_norm=final_norm),
        grid=(m // tm,),
        in_specs=[pl.BlockSpec((tm, d), lambda i: (i, 0)),
                  pl.BlockSpec((tm, dp), lambda i: (i, 0)),
                  pl.BlockSpec((1, d), lambda i: (0, 0)),
                  pl.BlockSpec((d, d), lambda i: (0, 0)),
                  pl.BlockSpec((dp, d), lambda i: (0, 0)),
                  pl.BlockSpec((1, d), lambda i: (0, 0))],
        out_specs=pl.BlockSpec((tm, d), lambda i: (i, 0)),
        out_shape=jax.ShapeDtypeStruct((m, d), F32),
        compiler_params=_params(("parallel",)),
        name="ple",
    )(x, p, g.reshape(1, d), w_gate.astype(BF16), w_proj.astype(BF16), final_g.reshape(1, d))


def _dsa_prep_kernel(a_ref, cqg_ref, ckvg_ref, wq_ref, ql_ref, qi_ref, kv_ref, ki_ref, wi_ref):
    a = a_ref[...]
    cqn = _rms(a[:, :A_DCQ], cqg_ref[...]).astype(BF16)
    q = jnp.dot(cqn, wq_ref[...], preferred_element_type=F32)
    nl = A_HEADS * A_DC
    ql_ref[...] = (q[:, :nl] * A_DC ** -0.5).astype(BF16)
    qi_ref[...] = q[:, nl:].astype(BF16)
    kv_ref[...] = _rms(a[:, A_DCQ:A_DCQ + A_DC], ckvg_ref[...]).astype(BF16)
    o = A_DCQ + A_DC
    ki_ref[...] = a[:, o:o + IDX_DIM].astype(BF16)
    wi_ref[...] = a[:, o + IDX_DIM:o + IDX_DIM + IDX_HEADS] * (IDX_HEADS * IDX_DIM) ** -0.5


def _dsa_prep(a_in, cq_g, ckv_g, w_q, tm=512):
    m, na = a_in.shape
    nq = w_q.shape[1]
    nl = A_HEADS * A_DC
    ni = IDX_HEADS * IDX_DIM
    row = lambda i: (i, 0)
    fixed = lambda i: (0, 0)
    return pl.pallas_call(
        _dsa_prep_kernel,
        grid=(m // tm,),
        in_specs=[pl.BlockSpec((tm, na), row),
                  pl.BlockSpec((1, A_DCQ), fixed),
                  pl.BlockSpec((1, A_DC), fixed),
                  pl.BlockSpec((A_DCQ, nq), fixed)],
        out_specs=[pl.BlockSpec((tm, nl), row), pl.BlockSpec((tm, ni), row),
                   pl.BlockSpec((tm, A_DC), row), pl.BlockSpec((tm, IDX_DIM), row),
                   pl.BlockSpec((tm, IDX_HEADS), row)],
        out_shape=[jax.ShapeDtypeStruct((m, nl), BF16), jax.ShapeDtypeStruct((m, ni), BF16),
                   jax.ShapeDtypeStruct((m, A_DC), BF16), jax.ShapeDtypeStruct((m, IDX_DIM), BF16),
                   jax.ShapeDtypeStruct((m, IDX_HEADS), F32)],
        compiler_params=_params(("parallel",)),
        name="dsa_prep",
    )(a_in, cq_g.reshape(1, -1), ckv_g.reshape(1, -1), w_q)


def _a_strip_off():
    return REL_MAX_DIST + A_KEY_TILE - Q_BLOCK


def _a_strip_len():
    return _a_strip_off() + A_KEY_TILE


def _a_strip_kernel(tbl_ref, o_ref):
    h = pl.program_id(0)
    shape = (Q_BLOCK, _a_strip_len())
    d = (lax.broadcasted_iota(I32, shape, 0) + _a_strip_off()
         - lax.broadcasted_iota(I32, shape, 1))
    bucket = _rel_bucket(d)
    far = tbl_ref[REL_BUCKETS - 1, h]
    val = jnp.zeros(shape, F32)
    for b in range(REL_BUCKETS - 1):
        val = jnp.where(bucket == b, tbl_ref[b, h] - far, val)
    o_ref[0] = val


def _a_strips(tbl):
    return pl.pallas_call(
        _a_strip_kernel,
        grid=(A_HEADS,),
        in_specs=[pl.BlockSpec(memory_space=pltpu.SMEM)],
        out_specs=pl.BlockSpec((1, Q_BLOCK, _a_strip_len()), lambda h: (h, 0, 0)),
        out_shape=jax.ShapeDtypeStruct((A_HEADS, Q_BLOCK, _a_strip_len()), F32),
        compiler_params=_params(("arbitrary",)),
        name="dsa_bias_strips",
    )(tbl)


def _c_bias_kernel(tbl_ref, o_ref, *, dilation, col0):
    v = pl.program_id(0)
    h = pl.program_id(1)
    shape = (Q_BLOCK, 2 * Q_BLOCK)
    i = lax.broadcasted_iota(I32, shape, 0)
    j = lax.broadcasted_iota(I32, shape, 1)
    dist = Q_BLOCK + i - j
    bucket = _rel_bucket(dist * dilation)
    val = jnp.zeros(shape, F32)
    for b in range(REL_BUCKETS):
        val = jnp.where(bucket == b, tbl_ref[b, col0 + h], val)
    valid = (dist >= 0) & (dist <= Q_BLOCK) & ((j >= Q_BLOCK) | (v > 0))
    o_ref[0, 0] = jnp.where(valid, val, NEG_BIG)


def _c_bias(tbl, dilation, col0):
    return pl.pallas_call(
        functools.partial(_c_bias_kernel, dilation=dilation, col0=col0),
        grid=(2, C_HEADS),
        in_specs=[pl.BlockSpec(memory_space=pltpu.SMEM)],
        out_specs=pl.BlockSpec((1, 1, Q_BLOCK, 2 * Q_BLOCK), lambda v, h: (v, h, 0, 0)),
        out_shape=jax.ShapeDtypeStruct((2, C_HEADS, Q_BLOCK, 2 * Q_BLOCK), F32),
        compiler_params=_params(("arbitrary", "arbitrary")),
        name="dilated_bias",
    )(tbl)


def _dsa_kernel(qi_ref, wi_ref, kit_ref, q_ref, kvt_ref, kva_ref, strip_ref, o_ref,
                sc_ref, thr_ref, m_ref, acc_ref, *, topk, seq):
    qb = pl.program_id(1)
    hg = pl.program_id(2)
    w_tile = A_KEY_TILE
    q0 = qb * Q_BLOCK
    n_tiles = q0 // w_tile + 1
    cols = w_tile // LANES
    lane_iota = lax.broadcasted_iota(I32, (Q_BLOCK, w_tile), 1)
    row_iota = lax.broadcasted_iota(I32, (Q_BLOCK, w_tile), 0)

    def tile_start(kt):
        return pl.multiple_of(kt * w_tile, w_tile)

    @pl.when(hg == 0)
    def _select():
        qi = qi_ref[0].reshape(IDX_HEADS * Q_BLOCK, IDX_DIM)
        wcols = [wi_ref[0, h] for h in range(IDX_HEADS)]

        def score_tile(kt, carry):
            k0 = tile_start(kt)
            r = jnp.dot(qi, kit_ref[0, :, pl.ds(k0, w_tile)], preferred_element_type=F32)
            r = jnp.maximum(r, 0.0).reshape(IDX_HEADS, Q_BLOCK, w_tile)
            s = r[0] * wcols[0]
            for h in range(1, IDX_HEADS):
                s = s + r[h] * wcols[h]
            s = jnp.where(k0 + lane_iota <= q0 + row_iota, s, -jnp.inf)
            bits = pltpu.bitcast(s, I32)
            sc_ref[:, pl.ds(k0, w_tile)] = bits ^ ((bits >> 31) & 0x7FFFFFFF)
            return carry

        lax.fori_loop(0, n_tiles, score_tile, 0)

        def count(pred):
            def body(kt, cnt):
                k0 = tile_start(kt)
                blk = sc_ref[:, pl.ds(k0, w_tile)]
                for c in range(cols):
                    hit = pred(blk[:, c * LANES:(c + 1) * LANES], k0 + c * LANES)
                    cnt = cnt + jnp.where(hit, 1, 0)
                return cnt
            cnt = lax.fori_loop(0, n_tiles, body, jnp.zeros((Q_BLOCK, LANES), I32))
            return jnp.sum(cnt, axis=1, keepdims=True)

        def bcast(v):
            return jnp.broadcast_to(v, (Q_BLOCK, LANES))

        def kth_bit(i, t):
            cand = t | lax.shift_left(jnp.int32(1), 31 - i)
            cand_b = bcast(cand ^ INT_MIN)
            c = count(lambda blk, _: blk >= cand_b)
            return jnp.where(c >= topk, cand, t)

        t = lax.fori_loop(0, 32, kth_bit, jnp.zeros((Q_BLOCK, 1), I32))
        thr = jnp.maximum(t ^ INT_MIN, KEY_NEG_INF + 1)
        thr_b = bcast(thr)
        thr_ref[:, 0:1] = thr
        n_gt = count(lambda blk, _: blk > thr_b)
        n_eq = count(lambda blk, _: blk == thr_b)
        need = topk - n_gt
        thr_ref[:, 1:2] = jnp.full((Q_BLOCK, 1), 2 * seq, I32)

        @pl.when(jnp.max(n_eq - need) > 0)
        def _ties():
            lane = lax.broadcasted_iota(I32, (Q_BLOCK, LANES), 1)
            nbits = (2 * seq - 1).bit_length()

            def lim_bit(i, lim):
                cand = lim | lax.shift_left(jnp.int32(1), nbits - 1 - i)
                cand_b = bcast(cand)
                c = count(lambda blk, k0: jnp.where(blk == thr_b, k0 + lane, 2 * seq) < cand_b)
                return jnp.where(c <= need, cand, lim)

            thr_ref[:, 1:2] = lax.fori_loop(0, nbits, lim_bit, jnp.zeros((Q_BLOCK, 1), I32))

        lim = thr_ref[:, 1:2]

        def mask_tile(kt, carry):
            k0 = tile_start(kt)
            blk = sc_ref[:, pl.ds(k0, w_tile)]
            bound = jnp.where(k0 + lane_iota < lim, thr, thr + 1)
            add = jnp.where(blk >= bound, 0.0, NEG_BIG).astype(F32)
            sc_ref[:, pl.ds(k0, w_tile)] = pltpu.bitcast(add, I32)
            return carry

        lax.fori_loop(0, n_tiles, mask_tile, 0)

    rows = A_HEAD_GROUP * Q_BLOCK
    q = q_ref[0].reshape(rows, A_DC)
    m_ref[...] = jnp.full(m_ref.shape, NEG_BIG, F32)
    acc_ref[...] = jnp.zeros_like(acc_ref)

    def attn_tile(kt, near):
        k0 = tile_start(kt)
        lg = jnp.dot(q, kvt_ref[0, :, pl.ds(k0, w_tile)], preferred_element_type=F32)
        x = lg.reshape(A_HEAD_GROUP, Q_BLOCK, w_tile)
        x = x + pltpu.bitcast(sc_ref[:, pl.ds(k0, w_tile)], F32)[None]
        if near:
            l0 = pl.multiple_of(_a_strip_off() - (q0 - k0), LANES)
            x = x + strip_ref[:, :, pl.ds(l0, w_tile)]
        x = x.reshape(rows, w_tile)
        m_old = m_ref[...]
        m_new = jnp.maximum(m_old, jnp.max(x, axis=1, keepdims=True))
        p = jnp.exp(x - m_new).astype(BF16)
        pv = jnp.dot(p, kva_ref[0, pl.ds(k0, w_tile), :], preferred_element_type=F32)
        acc_ref[...] = acc_ref[...] * jnp.exp(m_old - m_new) + pv
        m_ref[...] = m_new

    n_far = jnp.maximum(q0 - (REL_MAX_DIST - 1), 0) // w_tile

    def far_body(kt, carry):
        attn_tile(kt, False)
        return carry

    def near_body(kt, carry):
        attn_tile(kt, True)
        return carry

    lax.fori_loop(0, n_far, far_body, 0)
    lax.fori_loop(n_far, n_tiles, near_body, 0)

    acc = acc_ref[...]
    out = acc[:, :A_DC] * (1.0 / acc[:, A_DC:A_DC + 1])
    for hh in range(A_HEAD_GROUP):
        o_ref[0, :, hh * A_DC:(hh + 1) * A_DC] = out[hh * Q_BLOCK:(hh + 1) * Q_BLOCK].astype(o_ref.dtype)


def _dsa_attention(qi, wi, kit, q, kvt, kva, strips, topk):
    bsz, _, t, _ = q.shape
    groups = A_HEADS // A_HEAD_GROUP
    rows = A_HEAD_GROUP * Q_BLOCK
    resident = dict(pipeline_mode=pl.Buffered(1))
    return pl.pallas_call(
        functools.partial(_dsa_kernel, topk=topk, seq=t),
        grid=(bsz, t // Q_BLOCK, groups),
        in_specs=[
            pl.BlockSpec((1, IDX_HEADS, Q_BLOCK, IDX_DIM), lambda b, i, g: (b, 0, i, 0)),
            pl.BlockSpec((1, IDX_HEADS, Q_BLOCK, 1), lambda b, i, g: (b, 0, i, 0)),
            pl.BlockSpec((1, IDX_DIM, t), lambda b, i, g: (b, 0, 0), **resident),
            pl.BlockSpec((1, A_HEAD_GROUP, Q_BLOCK, A_DC), lambda b, i, g: (b, g, i, 0)),
            pl.BlockSpec((1, A_DC, t), lambda b, i, g: (b, 0, 0), **resident),
            pl.BlockSpec((1, t, 2 * A_DC), lambda b, i, g: (b, 0, 0), **resident),
            pl.BlockSpec((A_HEAD_GROUP, Q_BLOCK, _a_strip_len()), lambda b, i, g: (g, 0, 0)),
        ],
        out_specs=pl.BlockSpec((1, Q_BLOCK, A_HEAD_GROUP * A_DC), lambda b, i, g: (b, i, g)),
        out_shape=jax.ShapeDtypeStruct((bsz, t, A_HEADS * A_DC), BF16),
        scratch_shapes=[pltpu.VMEM((Q_BLOCK, t), I32),
                        pltpu.VMEM((Q_BLOCK, LANES), I32),
                        pltpu.VMEM((rows, 1), F32),
                        pltpu.VMEM((rows, 2 * A_DC), F32)],
        compiler_params=_params(("parallel", "arbitrary", "arbitrary")),
        name="dsa_attention",
    )(qi, wi, kit, q, kvt, kva, strips)


def _dsa_mixer(a_in, cq_g, ckv_g, w_uq, w_iq, strips, bsz, t):
    w_q = jnp.concatenate([w_uq, w_iq], axis=1).astype(BF16)
    ql, qi, kv, ki, wi = _dsa_prep(a_in, cq_g, ckv_g, w_q)
    ql = ql.reshape(bsz, t, A_HEADS, A_DC).transpose(0, 2, 1, 3)
    qi = qi.reshape(bsz, t, IDX_HEADS, IDX_DIM).transpose(0, 2, 1, 3)
    wi = wi.reshape(bsz, t, IDX_HEADS).transpose(0, 2, 1)[..., None]
    kv = kv.reshape(bsz, t, A_DC)
    kvt = kv.transpose(0, 2, 1)
    kva = jnp.concatenate([kv, jnp.ones((bsz, t, 1), BF16), jnp.zeros((bsz, t, A_DC - 1), BF16)], axis=-1)
    kit = ki.reshape(bsz, t, IDX_DIM).transpose(0, 2, 1)
    topk = min(TOPK_MAX, t // 4)
    return _dsa_attention(qi, wi, kit, ql, kvt, kva, strips, topk)


def _split3(x):
    hi = x.astype(BF16)
    r1 = x - hi.astype(F32)
    mid = r1.astype(BF16)
    lo = (r1 - mid.astype(F32)).astype(BF16)
    return hi, mid, lo


def _ssd_kernel(z_ref, xbc_ref, dt_ref, cw_ref, cb_ref, dtb_ref, alog_ref, dsk_ref, ng_ref, y_ref,
                ext_ref, st_ref, yb_ref, *, d_inner):
    c = pl.program_id(1)
    q = SSM_CHUNK
    n_state = SSM_STATE
    p_dim = SSM_HEADDIM
    hpg = d_inner // p_dim // SSM_GROUPS
    hist = 8

    @pl.when(c == 0)
    def _():
        ext_ref[0:hist, :] = jnp.zeros((hist, ext_ref.shape[1]), F32)
        st_ref[...] = jnp.zeros_like(st_ref)

    ext_ref[hist:hist + q, :] = xbc_ref[0]
    conv = cb_ref[...]
    for k in range(SSM_CONV):
        lo = hist - (SSM_CONV - 1) + k
        conv = conv + cw_ref[k:k + 1, :] * ext_ref[lo:lo + q, :]
    ext_ref[0:hist, :] = ext_ref[q:q + hist, :]
    u = conv * _sigmoid(conv)
    gn = SSM_GROUPS * n_state
    xs = u[:, :d_inner]
    bm = u[:, d_inner:d_inner + gn]
    cm = u[:, d_inner + gn:]

    raw = dt_ref[0] + dtb_ref[...]
    dt = jnp.maximum(raw, 0.0) + jnp.log(1.0 + jnp.exp(-jnp.abs(raw)))
    dta = dt * (-jnp.exp(alog_ref[...]))
    ii = lax.broadcasted_iota(I32, (q, q), 0)
    jj = lax.broadcasted_iota(I32, (q, q), 1)
    tril = ii >= jj
    tri = jnp.where(tril, 1.0, 0.0).astype(BF16)
    cum = sum(jnp.dot(tri, part, preferred_element_type=F32) for part in _split3(dta))
    cum_t = cum.T
    dt_t = dt.T
    last_t = cum_t[:, q - 1:q]
    ecum = jnp.exp(cum)
    wst_t = jnp.exp(last_t - cum_t) * dt_t
    elast = jnp.exp(last_t)

    for g in range(SSM_GROUPS):
        bc = bm[:, g * n_state:(g + 1) * n_state]
        cc = cm[:, g * n_state:(g + 1) * n_state]
        bc16 = bc.astype(BF16)
        cc16 = cc.astype(BF16)
        cb = lax.dot_general(cc16, bc16, (((1,), (1,)), ((), ())), preferred_element_type=F32)
        xs_t = xs[:, g * hpg * p_dim:(g + 1) * hpg * p_dim].T
        for hh in range(hpg):
            h = g * hpg + hh
            seg = cum[:, h:h + 1] - cum_t[h:h + 1, :]
            decay = jnp.exp(jnp.where(tril, seg, -jnp.inf))
            wgt = (decay * cb * dt_t[h:h + 1, :]).astype(BF16)
            xc16 = xs[:, h * p_dim:(h + 1) * p_dim].astype(BF16)
            st = st_ref[h]
            y_off = lax.dot_general(cc16, st.astype(BF16), (((1,), (1,)), ((), ())),
                                    preferred_element_type=F32)
            y = jnp.dot(wgt, xc16, preferred_element_type=F32) + y_off * ecum[:, h:h + 1]
            wx_t = (xs_t[hh * p_dim:(hh + 1) * p_dim, :] * wst_t[h:h + 1, :]).astype(BF16)
            st_ref[h] = st * elast[h:h + 1, :] + jnp.dot(wx_t, bc16, preferred_element_type=F32)
            yb_ref[:, h * p_dim:(h + 1) * p_dim] = y

    z = z_ref[0]
    y = (yb_ref[...] + xs * dsk_ref[...]) * (z * _sigmoid(z))
    gw = hpg * p_dim
    for g in range(SSM_GROUPS):
        yg = y[:, g * gw:(g + 1) * gw]
        yn = yg * lax.rsqrt(jnp.mean(yg * yg, axis=-1, keepdims=True) + EPS)
        y_ref[0, :, g * gw:(g + 1) * gw] = (yn * ng_ref[:, g * gw:(g + 1) * gw]).astype(y_ref.dtype)


def _ssd_mixer(z, xbc, dt, conv_w, conv_b, dt_bias, a_log, d_skip, norm_g):
    bsz, t, d_inner = z.shape
    conv_dim = xbc.shape[-1]
    heads = d_inner // SSM_HEADDIM
    pad = LANES - heads
    dtb = jnp.pad(dt_bias, (0, pad)).reshape(1, LANES)
    alog = jnp.pad(a_log, (0, pad)).reshape(1, LANES)
    dsk = jnp.repeat(d_skip, SSM_HEADDIM).reshape(1, d_inner)
    fixed = lambda b, c: (0, 0)
    return pl.pallas_call(
        functools.partial(_ssd_kernel, d_inner=d_inner),
        grid=(bsz, t // SSM_CHUNK),
        in_specs=[pl.BlockSpec((1, SSM_CHUNK, d_inner), lambda b, c: (b, c, 0)),
                  pl.BlockSpec((1, SSM_CHUNK, conv_dim), lambda b, c: (b, c, 0)),
                  pl.BlockSpec((1, SSM_CHUNK, LANES), lambda b, c: (b, c, 0)),
                  pl.BlockSpec((SSM_CONV, conv_dim), fixed),
                  pl.BlockSpec((1, conv_dim), fixed),
                  pl.BlockSpec((1, LANES), fixed),
                  pl.BlockSpec((1, LANES), fixed),
                  pl.BlockSpec((1, d_inner), fixed),
                  pl.BlockSpec((1, d_inner), fixed)],
        out_specs=pl.BlockSpec((1, SSM_CHUNK, d_inner), lambda b, c: (b, c, 0)),
        out_shape=jax.ShapeDtypeStruct((bsz, t, d_inner), BF16),
        scratch_shapes=[pltpu.VMEM((SSM_CHUNK + 8, conv_dim), F32),
                        pltpu.VMEM((heads, SSM_HEADDIM, SSM_STATE), F32),
                        pltpu.VMEM((SSM_CHUNK, d_inner), F32)],
        compiler_params=_params(("parallel", "arbitrary")),
        name="ssd",
    )(z, xbc, dt, conv_w.reshape(SSM_CONV, conv_dim), conv_b.reshape(1, conv_dim), dtb, alog, dsk,
      norm_g.reshape(1, d_inner))


def _dilated_kernel(q_ref, kc_ref, kp_ref, vc_ref, vp_ref, b_ref, o_ref, l_ref):
    d = C_HEADDIM
    q = q_ref[0]
    k = jnp.concatenate([kp_ref[0], kc_ref[0]], axis=0)
    v = jnp.concatenate([vp_ref[0], vc_ref[0]], axis=0)
    for h in range(C_HEADS):
        sl = slice(h * d, (h + 1) * d)
        lg = lax.dot_general(q[:, sl], k[:, sl], (((1,), (1,)), ((), ())), preferred_element_type=F32)
        x = lg * d ** -0.5 + b_ref[0, h]
        m = jnp.max(x, axis=1, keepdims=True)
        p = jnp.exp(x - m)
        s = jnp.sum(p, axis=1, keepdims=True)
        o = jnp.dot(p.astype(BF16), v[:, sl], preferred_element_type=F32)
        o_ref[0, :, sl] = o * (1.0 / s)
        l_ref[0, :, sl] = jnp.broadcast_to(m + jnp.log(s), (Q_BLOCK, d))


def _dilated_group(qkv, bias, g, dilation):
    bsz, t, cols = qkv.shape
    hd = C_HEADS * C_HEADDIM
    per_tok = cols // hd
    tr = t // dilation
    view = qkv.reshape(bsz, tr, dilation * cols)
    col = lambda c, j: c * per_tok + g * 3 + j
    cur = lambda j: (lambda b, c, n: (b, n, col(c, j)))
    prev = lambda j: (lambda b, c, n: (b, jnp.maximum(n - 1, 0), col(c, j)))
    blk = (1, Q_BLOCK, hd)
    out_spec = pl.BlockSpec(blk, lambda b, c, n: (b, n, c))
    o, lse = pl.pallas_call(
        _dilated_kernel,
        grid=(bsz, dilation, tr // Q_BLOCK),
        in_specs=[pl.BlockSpec(blk, cur(0)), pl.BlockSpec(blk, cur(1)), pl.BlockSpec(blk, prev(1)),
                  pl.BlockSpec(blk, cur(2)), pl.BlockSpec(blk, prev(2)),
                  pl.BlockSpec((1, C_HEADS, Q_BLOCK, 2 * Q_BLOCK),
                               lambda b, c, n: (jnp.minimum(n, 1), 0, 0, 0))],
        out_specs=[out_spec, out_spec],
        out_shape=[jax.ShapeDtypeStruct((bsz, tr, dilation * hd), F32)] * 2,
        compiler_params=_params(("parallel", "parallel", "arbitrary")),
        name="dilated_attention",
    )(view, view, view, view, view, bias)
    return o.reshape(bsz * t, hd), lse.reshape(bsz * t, hd)


def _merge_kernel(x_ref, ya_ref, yb_ref, o1_ref, o2_ref, o3_ref, l1_ref, l2_ref, l3_ref,
                  ga_ref, gb_ref, gc_ref, wa_ref, wb_ref, wc_ref, wo_ref, out_ref):
    l1, l2, l3 = l1_ref[...], l2_ref[...], l3_ref[...]
    lm = jnp.maximum(jnp.maximum(l1, l2), l3)
    e1, e2, e3 = jnp.exp(l1 - lm), jnp.exp(l2 - lm), jnp.exp(l3 - lm)
    yc = (e1 * o1_ref[...] + e2 * o2_ref[...] + e3 * o3_ref[...]) * (1.0 / (e1 + e2 + e3))
    pa = jnp.dot(ya_ref[...], wa_ref[...], preferred_element_type=F32)
    pb = jnp.dot(yb_ref[...], wb_ref[...], preferred_element_type=F32)
    pc = jnp.dot(yc.astype(BF16), wc_ref[...], preferred_element_type=F32)
    merged = _sigmoid(ga_ref[...]) * pa + _sigmoid(gb_ref[...]) * pb + _sigmoid(gc_ref[...]) * pc
    out_ref[...] = x_ref[...] + jnp.dot(merged.astype(BF16), wo_ref[...], preferred_element_type=F32)


def _merge(x, ya, yb, oc, lc, gates, w_a, w_b, w_c, w_o, tm=256):
    m, d = x.shape
    row = lambda i: (i, 0)
    fixed = lambda i: (0, 0)
    full = lambda a: pl.BlockSpec(a.shape, fixed)
    rows = lambda a: pl.BlockSpec((tm, a.shape[1]), row)
    weights = [w_a.astype(BF16), w_b.astype(BF16), w_c.astype(BF16), w_o.astype(BF16)]
    return pl.pallas_call(
        _merge_kernel,
        grid=(m // tm,),
        in_specs=[rows(x), rows(ya), rows(yb)] + [rows(a) for a in oc] + [rows(a) for a in lc]
                 + [pl.BlockSpec((tm, d), lambda i, j=j: (i, j)) for j in range(3)]
                 + [full(w) for w in weights],
        out_specs=pl.BlockSpec((tm, d), row),
        out_shape=jax.ShapeDtypeStruct((m, d), F32),
        compiler_params=_params(("parallel",)),
        name="merge",
    )(x, ya, yb, *oc, *lc, gates, gates, gates, *weights)


def _pad_cols(w, n):
    return jnp.pad(w, ((0, 0), (0, n - w.shape[1])))


def kernel(x, p, rel_bias, final_norm, ffn1_norm, ffn1_gate, ffn1_up, ffn1_down, mix_norm, w_in,
           cq_norm, ckv_norm, w_uq, w_iq, conv_w, conv_b, dt_bias, a_log, d_skip, ssm_norm,
           w_out_a, w_out_b, w_out_c, w_o, ffn2_norm, ffn2_gate, ffn2_up, ffn2_down, ple_norm,
           w_ple_gate, w_ple_proj):
    bsz, t, d_model = x.shape
    depth = w_in.shape[0]
    m = bsz * t
    d_inner = ssm_norm.shape[1]
    conv_dim = conv_b.shape[1]
    ssm_heads = dt_bias.shape[1]
    c_cols = len(C_GROUPS) * 3 * C_HEADS * C_HEADDIM
    sizes = (A_DCQ, A_DC, IDX_DIM, IDX_HEADS, d_inner, conv_dim, ssm_heads, c_cols,
             d_model, d_model, d_model)
    offs = [0]
    for s in sizes:
        offs.append(offs[-1] + s)

    strips = _a_strips(rel_bias)
    c_bias = [_c_bias(rel_bias, dil, A_HEADS + g * C_HEADS) for g, (_, dil) in enumerate(C_GROUPS)]

    x = x.reshape(m, d_model)
    for i in range(depth):
        x = _ffn(x, ffn1_norm[i], ffn1_gate[i], ffn1_up[i], ffn1_down[i])

        w = w_in[i].astype(BF16)
        g_mix = mix_norm[i]
        a_in = _norm_matmul(x, g_mix, _pad_cols(w[:, offs[0]:offs[4]], 4 * LANES), F32, 512, 4 * LANES)
        z = _norm_matmul(x, g_mix, w[:, offs[4]:offs[5]], F32, 512, 1024)
        xbc = _norm_matmul(x, g_mix, w[:, offs[5]:offs[6]], F32, 512, 1024)
        dt = _norm_matmul(x, g_mix, _pad_cols(w[:, offs[6]:offs[7]], LANES), F32, 512, LANES)
        qkv = _norm_matmul(x, g_mix, w[:, offs[7]:offs[8]], BF16, 512, 1536)
        gates = _norm_matmul(x, g_mix, w[:, offs[8]:offs[11]], F32, 512, 1024)

        ya = _dsa_mixer(a_in, cq_norm[i], ckv_norm[i], w_uq[i], w_iq[i], strips, bsz, t)
        yb = _ssd_mixer(z.reshape(bsz, t, d_inner), xbc.reshape(bsz, t, conv_dim),
                        dt.reshape(bsz, t, LANES), conv_w[i], conv_b[i], dt_bias[i], a_log[i],
                        d_skip[i], ssm_norm[i])
        qkv = qkv.reshape(bsz, t, c_cols)
        oc, lc = zip(*[_dilated_group(qkv, c_bias[g], g, dil) for g, (_, dil) in enumerate(C_GROUPS)])
        x = _merge(x, ya.reshape(m, -1), yb.reshape(m, -1), oc, lc, gates,
                   w_out_a[i], w_out_b[i], w_out_c[i], w_o[i])

        x = _ffn(x, ffn2_norm[i], ffn2_gate[i], ffn2_up[i], ffn2_down[i])
        x = _ple(x, p[i].reshape(m, -1), ple_norm[i], w_ple_gate[i], w_ple_proj[i], final_norm,
                 final_norm=(i == depth - 1))
    return x.reshape(bsz, t, d_model)
```

```python
import functools
import math

import jax
import jax.numpy as jnp
from jax import lax
from jax.experimental import pallas as pl
from jax.experimental.pallas import tpu as pltpu

F32 = jnp.float32
BF16 = jnp.bfloat16
I32 = jnp.int32

EPS = 1e-6
LANES = 128
Q_BLOCK = 128
VMEM_LIMIT = 56 * 1024 * 1024

A_HEADS = 16
A_DC = 128
A_DCQ = 256
IDX_HEADS = 8
IDX_DIM = 64
TOPK_MAX = 256
A_HEAD_GROUP = 4
A_KEY_TILE = 512
SSM_HEADDIM = 64
SSM_GROUPS = 8
SSM_STATE = 128
SSM_CONV = 4
SSM_CHUNK = 128
C_GROUPS = ((128, 1), (512, 4), (2048, 16))
C_HEADS = 8
C_HEADDIM = 64
REL_BUCKETS = 32
REL_MAX_DIST = 2048

LOG2E = 1.4426950408889634
NEG_BIG = -1e30
KEY_NEG_INF = -2139095041
INT_MIN = -2147483648


def _params(sem, vmem=VMEM_LIMIT):
    return pltpu.CompilerParams(dimension_semantics=sem, vmem_limit_bytes=vmem)


def _rms(x, g):
    return x * lax.rsqrt(jnp.mean(x * x, axis=-1, keepdims=True) + EPS) * g


def _sigmoid(x):
    return 1.0 / (1.0 + jnp.exp(-x))


def _rel_bucket(d):
    max_exact = REL_BUCKETS // 2
    d = jnp.maximum(d, 0)
    df = jnp.maximum(d, 1).astype(F32)
    large = max_exact + (jnp.log(df / max_exact) / math.log(REL_MAX_DIST / max_exact)
                         * (REL_BUCKETS - max_exact)).astype(I32)
    large = jnp.minimum(large, REL_BUCKETS - 1)
    return jnp.where(d < max_exact, d, large)


def _norm_matmul_kernel(x_ref, g_ref, w_ref, o_ref, hn_ref):
    @pl.when(pl.program_id(1) == 0)
    def _():
        hn_ref[...] = _rms(x_ref[...], g_ref[...]).astype(BF16)

    o_ref[...] = jnp.dot(hn_ref[...], w_ref[...], preferred_element_type=F32).astype(o_ref.dtype)


def _norm_matmul(x, g, w, out_dtype, tm, tn):
    m, k = x.shape
    n = w.shape[1]
    return pl.pallas_call(
        _norm_matmul_kernel,
        grid=(m // tm, n // tn),
        in_specs=[pl.BlockSpec((tm, k), lambda i, j: (i, 0)),
                  pl.BlockSpec((1, k), lambda i, j: (0, 0)),
                  pl.BlockSpec((k, tn), lambda i, j: (0, j))],
        out_specs=pl.BlockSpec((tm, tn), lambda i, j: (i, j)),
        out_shape=jax.ShapeDtypeStruct((m, n), out_dtype),
        scratch_shapes=[pltpu.VMEM((tm, k), BF16)],
        compiler_params=_params(("parallel", "arbitrary")),
        name="norm_matmul",
    )(x, g.reshape(1, k), w)


def _ffn_kernel(x_ref, g_ref, wg_ref, wu_ref, wd_ref, o_ref, hn_ref, acc_ref):
    j = pl.program_id(1)

    @pl.when(j == 0)
    def _():
        hn_ref[...] = _rms(x_ref[...], g_ref[...]).astype(BF16)
        acc_ref[...] = jnp.zeros_like(acc_ref)

    h = hn_ref[...]
    gate = jnp.dot(h, wg_ref[...], preferred_element_type=F32)
    up = jnp.dot(h, wu_ref[...], preferred_element_type=F32)
    act = (gate * _sigmoid(gate) * up).astype(BF16)
    acc_ref[...] += jnp.dot(act, wd_ref[...], preferred_element_type=F32)

    @pl.when(j == pl.num_programs(1) - 1)
    def _():
        o_ref[...] = x_ref[...] + 0.5 * acc_ref[...]


def _ffn(x, g, w_gate, w_up, w_down, tm=512):
    m, d = x.shape
    f = w_gate.shape[1]
    tf = f // 2 if (f // 2) % LANES == 0 else f
    return pl.pallas_call(
        _ffn_kernel,
        grid=(m // tm, f // tf),
        in_specs=[pl.BlockSpec((tm, d), lambda i, j: (i, 0)),
                  pl.BlockSpec((1, d), lambda i, j: (0, 0)),
                  pl.BlockSpec((d, tf), lambda i, j: (0, j)),
                  pl.BlockSpec((d, tf), lambda i, j: (0, j)),
                  pl.BlockSpec((tf, d), lambda i, j: (j, 0))],
        out_specs=pl.BlockSpec((tm, d), lambda i, j: (i, 0)),
        out_shape=jax.ShapeDtypeStruct((m, d), F32),
        scratch_shapes=[pltpu.VMEM((tm, d), BF16), pltpu.VMEM((tm, d), F32)],
        compiler_params=_params(("parallel", "arbitrary")),
        name="ffn",
    )(x, g.reshape(1, d), w_gate.astype(BF16), w_up.astype(BF16), w_down.astype(BF16))


def _ple_kernel(x_ref, p_ref, g_ref, wg_ref, wp_ref, fg_ref, o_ref, *, final_norm):
    x = x_ref[...]
    h = _rms(x, g_ref[...]).astype(BF16)
    gate = _sigmoid(jnp.dot(h, wg_ref[...], preferred_element_type=F32))
    proj = jnp.dot(p_ref[...].astype(BF16), wp_ref[...], preferred_element_type=F32)
    y = x + gate * proj
    if final_norm:
        y = _rms(y, fg_ref[...])
    o_ref[...] = y


def _ple(x, p, g, w_gate, w_proj, final_g, final_norm, tm=512):
    m, d = x.shape
    dp = p.shape[1]
    return pl.pallas_call(
        functools.partial(_ple_kernel, final_norm=final_norm),
        grid=(m // tm,),
        in_specs=[pl.BlockSpec((tm, d), lambda i: (i, 0)),
                  pl.BlockSpec((tm, dp), lambda i: (i, 0)),
                  pl.BlockSpec((1, d), lambda i: (0, 0)),
                  pl.BlockSpec((d, d), lambda i: (0, 0)),
                  pl.BlockSpec((dp, d), lambda i: (0, 0)),
                  pl.BlockSpec((1, d), lambda i: (0, 0))],
        out_specs=pl.BlockSpec((tm, d), lambda i: (i, 0)),
        out_shape=jax.ShapeDtypeStruct((m, d), F32),
        compiler_params=_params(("parallel",)),
        name="ple",
    )(x, p, g.reshape(1, d), w_gate.astype(BF16), w_proj.astype(BF16), final_g.reshape(1, d))


def _dsa_prep_kernel(a_ref, cqg_ref, ckvg_ref, wq_ref, ql_ref, qi_ref, kvt_ref, kva_ref, kit_ref,
                     wit_ref):
    a = a_ref[0]
    tm = a.shape[0]
    cqn = _rms(a[:, :A_DCQ], cqg_ref[...]).astype(BF16)
    q = jnp.dot(cqn, wq_ref[...], preferred_element_type=F32)
    nl = A_HEADS * A_DC
    ql = (q[:, :nl] * (A_DC ** -0.5 * LOG2E)).astype(BF16)
    for h in range(A_HEADS):
        ql_ref[0, h] = ql[:, h * A_DC:(h + 1) * A_DC]
    qi = q[:, nl:].astype(BF16)
    for h in range(IDX_HEADS):
        qi_ref[0, h] = qi[:, h * IDX_DIM:(h + 1) * IDX_DIM]
    kv = _rms(a[:, A_DCQ:A_DCQ + A_DC], ckvg_ref[...])
    kvt_ref[0] = kv.T.astype(BF16)
    lane = lax.broadcasted_iota(I32, (tm, A_DC), 1)
    kva_ref[0, :, :A_DC] = kv.astype(BF16)
    kva_ref[0, :, A_DC:] = jnp.where(lane == 0, 1.0, 0.0).astype(BF16)
    o = A_DCQ + A_DC
    rest_t = a[:, o:o + LANES].T
    kit_ref[0] = rest_t[:IDX_DIM].astype(BF16)
    wit_ref[0] = rest_t[IDX_DIM:IDX_DIM + IDX_HEADS] * (IDX_HEADS * IDX_DIM) ** -0.5


def _dsa_prep(a_in, cq_g, ckv_g, w_q, tm=512):
    bsz, t, na = a_in.shape
    nq = w_q.shape[1]
    fixed = lambda b, i: (0, 0)
    return pl.pallas_call(
        _dsa_prep_kernel,
        grid=(bsz, t // tm),
        in_specs=[pl.BlockSpec((1, tm, na), lambda b, i: (b, i, 0)),
                  pl.BlockSpec((1, A_DCQ), fixed),
                  pl.BlockSpec((1, A_DC), fixed),
                  pl.BlockSpec((A_DCQ, nq), fixed)],
        out_specs=[pl.BlockSpec((1, A_HEADS, tm, A_DC), lambda b, i: (b, 0, i, 0)),
                   pl.BlockSpec((1, IDX_HEADS, tm, IDX_DIM), lambda b, i: (b, 0, i, 0)),
                   pl.BlockSpec((1, A_DC, tm), lambda b, i: (b, 0, i)),
                   pl.BlockSpec((1, tm, 2 * A_DC), lambda b, i: (b, i, 0)),
                   pl.BlockSpec((1, IDX_DIM, tm), lambda b, i: (b, 0, i)),
                   pl.BlockSpec((1, IDX_HEADS, tm), lambda b, i: (b, 0, i))],
        out_shape=[jax.ShapeDtypeStruct((bsz, A_HEADS, t, A_DC), BF16),
                   jax.ShapeDtypeStruct((bsz, IDX_HEADS, t, IDX_DIM), BF16),
                   jax.ShapeDtypeStruct((bsz, A_DC, t), BF16),
                   jax.ShapeDtypeStruct((bsz, t, 2 * A_DC), BF16),
                   jax.ShapeDtypeStruct((bsz, IDX_DIM, t), BF16),
                   jax.ShapeDtypeStruct((bsz, IDX_HEADS, t), F32)],
        compiler_params=_params(("parallel", "parallel")),
        name="dsa_prep",
    )(a_in, cq_g.reshape(1, -1), ckv_g.reshape(1, -1), w_q)


def _a_strip_off():
    return REL_MAX_DIST + A_KEY_TILE


def _a_strip_len():
    return _a_strip_off() + A_KEY_TILE


def _a_strip_kernel(tbl_ref, o_ref):
    h = pl.program_id(0)
    shape = (Q_BLOCK, _a_strip_len())
    d = (lax.broadcasted_iota(I32, shape, 0) + _a_strip_off()
         - lax.broadcasted_iota(I32, shape, 1))
    bucket = _rel_bucket(d)
    far = tbl_ref[REL_BUCKETS - 1, h]
    val = jnp.zeros(shape, F32)
    for b in range(REL_BUCKETS - 1):
        val = jnp.where(bucket == b, (tbl_ref[b, h] - far) * LOG2E, val)
    o_ref[0] = val


def _a_strips(tbl):
    return pl.pallas_call(
        _a_strip_kernel,
        grid=(A_HEADS,),
        in_specs=[pl.BlockSpec(memory_space=pltpu.SMEM)],
        out_specs=pl.BlockSpec((1, Q_BLOCK, _a_strip_len()), lambda h: (h, 0, 0)),
        out_shape=jax.ShapeDtypeStruct((A_HEADS, Q_BLOCK, _a_strip_len()), F32),
        compiler_params=_params(("arbitrary",)),
        name="dsa_bias_strips",
    )(tbl)


def _c_bias_kernel(tbl_ref, o_ref, *, dilation, col0):
    v = pl.program_id(0)
    h = pl.program_id(1)
    shape = (Q_BLOCK, 2 * Q_BLOCK)
    i = lax.broadcasted_iota(I32, shape, 0)
    j = lax.broadcasted_iota(I32, shape, 1)
    dist = Q_BLOCK + i - j
    bucket = _rel_bucket(dist * dilation)
    val = jnp.zeros(shape, F32)
    for b in range(REL_BUCKETS):
        val = jnp.where(bucket == b, tbl_ref[b, col0 + h], val)
    valid = (dist >= 0) & (dist <= Q_BLOCK) & ((j >= Q_BLOCK) | (v > 0))
    o_ref[0, 0] = jnp.where(valid, val, NEG_BIG)


def _c_bias(tbl, dilation, col0):
    return pl.pallas_call(
        functools.partial(_c_bias_kernel, dilation=dilation, col0=col0),
        grid=(2, C_HEADS),
        in_specs=[pl.BlockSpec(memory_space=pltpu.SMEM)],
        out_specs=pl.BlockSpec((1, 1, Q_BLOCK, 2 * Q_BLOCK), lambda v, h: (v, h, 0, 0)),
        out_shape=jax.ShapeDtypeStruct((2, C_HEADS, Q_BLOCK, 2 * Q_BLOCK), F32),
        compiler_params=_params(("arbitrary", "arbitrary")),
        name="dilated_bias",
    )(tbl)


def _dsa_kernel(qi_ref, wi_ref, kit_ref, q_ref, kvt_ref, kva_ref, strip_ref, o_ref,
                sc_ref, thr_ref, kaug_ref, qa_ref, m_ref, acc_ref, lg_ref, p_ref, al_ref, *, topk, seq):
    qb = pl.program_id(1)
    hg = pl.program_id(2)
    w_tile = A_KEY_TILE
    q0 = qb * Q_BLOCK
    n_tiles = q0 // w_tile + 1
    n_pairs = (n_tiles + 1) // 2
    cols = w_tile // LANES
    rows = A_HEAD_GROUP * Q_BLOCK

    @pl.when((qb == 0) & (hg == 0))
    def _load_keys():
        kaug_ref[:A_DC, :] = kvt_ref[0]
    lane_iota = lax.broadcasted_iota(I32, (Q_BLOCK, w_tile), 1)
    row_iota = lax.broadcasted_iota(I32, (Q_BLOCK, w_tile), 0)

    def tile_start(kt):
        return pl.multiple_of(kt * w_tile, w_tile)

    @pl.when(hg == 0)
    def _select():
        qi = qi_ref[0].reshape(IDX_HEADS * Q_BLOCK, IDX_DIM)
        wcols = [wi_ref[0, h] for h in range(IDX_HEADS)]

        def score_tile(kt, carry):
            k0 = tile_start(kt)
            r = jnp.dot(qi, kit_ref[0, :, pl.ds(k0, w_tile)], preferred_element_type=F32)
            r = jnp.maximum(r, 0.0).reshape(IDX_HEADS, Q_BLOCK, w_tile)
            s = r[0] * wcols[0]
            for h in range(1, IDX_HEADS):
                s = s + r[h] * wcols[h]
            s = jnp.where(k0 + lane_iota <= q0 + row_iota, s, -jnp.inf)
            bits = pltpu.bitcast(s, I32)
            sc_ref[:, pl.ds(k0, w_tile)] = bits ^ ((bits >> 31) & 0x7FFFFFFF)
            return carry

        lax.fori_loop(0, n_tiles, score_tile, 0)

        def count(pred):
            def body(kt, cnt):
                k0 = tile_start(kt)
                blk = sc_ref[:, pl.ds(k0, w_tile)]
                for c in range(cols):
                    hit = pred(blk[:, c * LANES:(c + 1) * LANES], k0 + c * LANES)
                    cnt = cnt + jnp.where(hit, 1, 0)
                return cnt
            cnt = lax.fori_loop(0, n_tiles, body, jnp.zeros((Q_BLOCK, LANES), I32))
            return jnp.sum(cnt, axis=1, keepdims=True)

        def bcast(v):
            return jnp.broadcast_to(v, (Q_BLOCK, LANES))

        def kth_bit(i, t):
            cand = t | lax.shift_left(jnp.int32(1), 31 - i)
            cand_b = bcast(cand ^ INT_MIN)
            c = count(lambda blk, _: blk >= cand_b)
            return jnp.where(c >= topk, cand, t)

        t = lax.fori_loop(0, 32, kth_bit, jnp.zeros((Q_BLOCK, 1), I32))
        thr = jnp.maximum(t ^ INT_MIN, KEY_NEG_INF + 1)
        thr_b = bcast(thr)
        thr_ref[:, 0:1] = thr
        n_gt = count(lambda blk, _: blk > thr_b)
        n_eq = count(lambda blk, _: blk == thr_b)
        need = topk - n_gt
        thr_ref[:, 1:2] = jnp.full((Q_BLOCK, 1), 2 * seq, I32)

        @pl.when(jnp.max(n_eq - need) > 0)
        def _ties():
            lane = lax.broadcasted_iota(I32, (Q_BLOCK, LANES), 1)
            nbits = (2 * seq - 1).bit_length()

            def lim_bit(i, lim):
                cand = lim | lax.shift_left(jnp.int32(1), nbits - 1 - i)
                cand_b = bcast(cand)
                c = count(lambda blk, k0: jnp.where(blk == thr_b, k0 + lane, 2 * seq) < cand_b)
                return jnp.where(c <= need, cand, lim)

            thr_ref[:, 1:2] = lax.fori_loop(0, nbits, lim_bit, jnp.zeros((Q_BLOCK, 1), I32))

        lim = thr_ref[:, 1:2]

        def mask_tile(kt, carry):
            k0 = tile_start(kt)
            blk = sc_ref[:, pl.ds(k0, w_tile)]
            bound = jnp.where(k0 + lane_iota < lim, thr, thr + 1)
            add = jnp.where(blk >= bound, 0.0, NEG_BIG).astype(F32)
            add = jnp.where(kt < n_tiles, add, NEG_BIG)
            kaug_ref[A_DC:, pl.ds(k0, w_tile)] = add.astype(BF16)
            return carry

        lax.fori_loop(0, 2 * n_pairs, mask_tile, 0)

    chunk = 32
    eye = (lax.broadcasted_iota(I32, (Q_BLOCK, Q_BLOCK), 0)
           == lax.broadcasted_iota(I32, (Q_BLOCK, Q_BLOCK), 1))
    onehot = jnp.where(eye, 1.0, 0.0).astype(BF16)
    for hh in range(A_HEAD_GROUP):
        qa_ref[hh * Q_BLOCK:(hh + 1) * Q_BLOCK, :A_DC] = q_ref[0, hh]
        qa_ref[hh * Q_BLOCK:(hh + 1) * Q_BLOCK, A_DC:] = onehot
    m_ref[...] = jnp.full(m_ref.shape, NEG_BIG, F32)
    acc_ref[...] = jnp.zeros_like(acc_ref)

    def stage_a(kt, slot):
        k0 = tile_start(kt)
        lg_ref[slot] = jnp.dot(qa_ref[...], kaug_ref[:, pl.ds(k0, w_tile)], preferred_element_type=F32)

    def stage_b(kt, slot, near):
        k0 = tile_start(kt)
        l0 = pl.multiple_of(jnp.clip(_a_strip_off() - (q0 - k0), 0, _a_strip_off()), LANES)
        for hh in range(A_HEAD_GROUP):
            for r0 in range(0, Q_BLOCK, chunk):
                rs = pl.ds(hh * Q_BLOCK + r0, chunk)
                x = lg_ref[slot, rs, :]
                if near:
                    x = x + strip_ref[hh, pl.ds(r0, chunk), pl.ds(l0, w_tile)]
                m_old = m_ref[rs, :]
                m_new = jnp.maximum(m_old, jnp.max(x, axis=1, keepdims=True))
                for c in range(cols):
                    cs = slice(c * LANES, (c + 1) * LANES)
                    p_ref[slot, rs, cs] = jnp.exp2(x[:, cs] - m_new).astype(BF16)
                al_ref[slot, rs, :] = jnp.exp2(m_old - m_new)
                m_ref[rs, :] = m_new

    def stage_c(kt, slot):
        k0 = tile_start(kt)
        pv = jnp.dot(p_ref[slot], kva_ref[0, pl.ds(k0, w_tile), :], preferred_element_type=F32)
        al = al_ref[slot]
        acc_ref[:, :A_DC] = acc_ref[:, :A_DC] * al + pv[:, :A_DC]
        acc_ref[:, A_DC:] = acc_ref[:, A_DC:] * al + pv[:, A_DC:]

    stage_a(0, 0)
    stage_a(1, 1)
    stage_b(0, 0, True)

    def pair_body(i, near):
        kt = 2 * i
        stage_a(kt, 0)
        stage_b(kt - 1, 1, near)
        stage_c(kt - 2, 0)
        stage_a(kt + 1, 1)
        stage_b(kt, 0, near)
        stage_c(kt - 1, 1)

    def far_pair(i, carry):
        pair_body(i, False)
        return carry

    def near_pair(i, carry):
        pair_body(i, True)
        return carry

    n_far = jnp.maximum(q0 - (REL_MAX_DIST - 1), 0) // w_tile
    first_near = jnp.maximum((n_far + 1) // 2, 1)
    lax.fori_loop(1, first_near, far_pair, 0)
    lax.fori_loop(first_near, n_pairs, near_pair, 0)
    last = 2 * n_pairs - 1
    stage_b(last, 1, True)
    stage_c(last - 1, 0)
    stage_c(last, 1)

    acc = acc_ref[...]
    out = acc[:, :A_DC] * (1.0 / acc[:, A_DC:A_DC + 1])
    for hh in range(A_HEAD_GROUP):
        o_ref[0, :, hh * A_DC:(hh + 1) * A_DC] = out[hh * Q_BLOCK:(hh + 1) * Q_BLOCK].astype(o_ref.dtype)


def _dsa_attention(qi, wi, kit, q, kvt, kva, strips, topk):
    bsz, _, t, _ = q.shape
    assert t % (2 * A_KEY_TILE) == 0
    groups = A_HEADS // A_HEAD_GROUP
    rows = A_HEAD_GROUP * Q_BLOCK
    resident = dict(pipeline_mode=pl.Buffered(1))
    return pl.pallas_call(
        functools.partial(_dsa_kernel, topk=topk, seq=t),
        grid=(bsz, t // Q_BLOCK, groups),
        in_specs=[
            pl.BlockSpec((1, IDX_HEADS, Q_BLOCK, IDX_DIM), lambda b, i, g: (b, 0, i, 0)),
            pl.BlockSpec((1, IDX_HEADS, Q_BLOCK, 1), lambda b, i, g: (b, 0, i, 0)),
            pl.BlockSpec((1, IDX_DIM, t), lambda b, i, g: (b, 0, 0), **resident),
            pl.BlockSpec((1, A_HEAD_GROUP, Q_BLOCK, A_DC), lambda b, i, g: (b, g, i, 0)),
            pl.BlockSpec((1, A_DC, t), lambda b, i, g: (b, 0, 0), **resident),
            pl.BlockSpec((1, t, 2 * A_DC), lambda b, i, g: (b, 0, 0), **resident),
            pl.BlockSpec((A_HEAD_GROUP, Q_BLOCK, _a_strip_len()), lambda b, i, g: (g, 0, 0)),
        ],
        out_specs=pl.BlockSpec((1, Q_BLOCK, A_HEAD_GROUP * A_DC), lambda b, i, g: (b, i, g)),
        out_shape=jax.ShapeDtypeStruct((bsz, t, A_HEADS * A_DC), BF16),
        scratch_shapes=[pltpu.VMEM((Q_BLOCK, t), I32),
                        pltpu.VMEM((Q_BLOCK, LANES), I32),
                        pltpu.VMEM((2 * A_DC, t), BF16),
                        pltpu.VMEM((rows, 2 * A_DC), BF16),
                        pltpu.VMEM((rows, LANES), F32),
                        pltpu.VMEM((rows, 2 * A_DC), F32),
                        pltpu.VMEM((2, rows, A_KEY_TILE), F32),
                        pltpu.VMEM((2, rows, A_KEY_TILE), BF16),
                        pltpu.VMEM((2, rows, LANES), F32)],
        compiler_params=_params(("parallel", "arbitrary", "arbitrary")),
        name="dsa_attention",
    )(qi, wi, kit, q, kvt, kva, strips)


def _dsa_mixer(a_in, cq_g, ckv_g, w_uq, w_iq, strips, bsz, t):
    w_q = jnp.concatenate([w_uq, w_iq], axis=1).astype(BF16)
    ql, qi, kvt, kva, kit, wit = _dsa_prep(a_in.reshape(bsz, t, -1), cq_g, ckv_g, w_q)
    topk = min(TOPK_MAX, t // 4)
    return _dsa_attention(qi, wit[..., None], kit, ql, kvt, kva, strips, topk)


def _split3(x):
    hi = x.astype(BF16)
    r1 = x - hi.astype(F32)
    mid = r1.astype(BF16)
    lo = (r1 - mid.astype(F32)).astype(BF16)
    return hi, mid, lo


def _ssd_kernel(z_ref, xbc_ref, dt_ref, cw_ref, cb_ref, dtb_ref, alog_ref, dsk_ref, ng_ref, y_ref,
                ext_ref, st_ref, yb_ref, *, d_inner):
    c = pl.program_id(1)
    q = SSM_CHUNK
    n_state = SSM_STATE
    p_dim = SSM_HEADDIM
    hpg = d_inner // p_dim // SSM_GROUPS
    hist = 8

    @pl.when(c == 0)
    def _():
        ext_ref[0:hist, :] = jnp.zeros((hist, ext_ref.shape[1]), F32)
        st_ref[...] = jnp.zeros_like(st_ref)

    ext_ref[hist:hist + q, :] = xbc_ref[0]
    conv = cb_ref[...]
    for k in range(SSM_CONV):
        lo = hist - (SSM_CONV - 1) + k
        conv = conv + cw_ref[k:k + 1, :] * ext_ref[lo:lo + q, :]
    ext_ref[0:hist, :] = ext_ref[q:q + hist, :]
    u = conv * _sigmoid(conv)
    gn = SSM_GROUPS * n_state
    xs = u[:, :d_inner]
    bm = u[:, d_inner:d_inner + gn]
    cm = u[:, d_inner + gn:]

    raw = dt_ref[0] + dtb_ref[...]
    dt = jnp.maximum(raw, 0.0) + jnp.log(1.0 + jnp.exp(-jnp.abs(raw)))
    dta = dt * (-jnp.exp(alog_ref[...]))
    ii = lax.broadcasted_iota(I32, (q, q), 0)
    jj = lax.broadcasted_iota(I32, (q, q), 1)
    tril = ii >= jj
    tri = jnp.where(tril, 1.0, 0.0).astype(BF16)
    cum = sum(jnp.dot(tri, part, preferred_element_type=F32) for part in _split3(dta))
    cum_t = cum.T
    dt_t = dt.T
    last_t = cum_t[:, q - 1:q]
    ecum = jnp.exp(cum)
    wst_t = jnp.exp(last_t - cum_t) * dt_t
    elast = jnp.exp(last_t)

    for g in range(SSM_GROUPS):
        bc = bm[:, g * n_state:(g + 1) * n_state]
        cc = cm[:, g * n_state:(g + 1) * n_state]
        bc16 = bc.astype(BF16)
        cc16 = cc.astype(BF16)
        cb = lax.dot_general(cc16, bc16, (((1,), (1,)), ((), ())), preferred_element_type=F32)
        xs_t = xs[:, g * hpg * p_dim:(g + 1) * hpg * p_dim].T
        for hh in range(hpg):
            h = g * hpg + hh
            seg = cum[:, h:h + 1] - cum_t[h:h + 1, :]
            decay = jnp.exp(jnp.where(tril, seg, -jnp.inf))
            wgt = (decay * cb * dt_t[h:h + 1, :]).astype(BF16)
            xc16 = xs[:, h * p_dim:(h + 1) * p_dim].astype(BF16)
            st = st_ref[h]
            y_off = lax.dot_general(cc16, st.astype(BF16), (((1,), (1,)), ((), ())),
                                    preferred_element_type=F32)
            y = jnp.dot(wgt, xc16, preferred_element_type=F32) + y_off * ecum[:, h:h + 1]
            wx_t = (xs_t[hh * p_dim:(hh + 1) * p_dim, :] * wst_t[h:h + 1, :]).astype(BF16)
            st_ref[h] = st * elast[h:h + 1, :] + jnp.dot(wx_t, bc16, preferred_element_type=F32)
            yb_ref[:, h * p_dim:(h + 1) * p_dim] = y

    z = z_ref[0]
    y = (yb_ref[...] + xs * dsk_ref[...]) * (z * _sigmoid(z))
    gw = hpg * p_dim
    for g in range(SSM_GROUPS):
        yg = y[:, g * gw:(g + 1) * gw]
        yn = yg * lax.rsqrt(jnp.mean(yg * yg, axis=-1, keepdims=True) + EPS)
        y_ref[0, :, g * gw:(g + 1) * gw] = (yn * ng_ref[:, g * gw:(g + 1) * gw]).astype(y_ref.dtype)


def _ssd_mixer(z, xbc, dt, conv_w, conv_b, dt_bias, a_log, d_skip, norm_g):
    bsz, t, d_inner = z.shape
    conv_dim = xbc.shape[-1]
    heads = d_inner // SSM_HEADDIM
    pad = LANES - heads
    dtb = jnp.pad(dt_bias, (0, pad)).reshape(1, LANES)
    alog = jnp.pad(a_log, (0, pad)).reshape(1, LANES)
    dsk = jnp.repeat(d_skip, SSM_HEADDIM).reshape(1, d_inner)
    fixed = lambda b, c: (0, 0)
    return pl.pallas_call(
        functools.partial(_ssd_kernel, d_inner=d_inner),
        grid=(bsz, t // SSM_CHUNK),
        in_specs=[pl.BlockSpec((1, SSM_CHUNK, d_inner), lambda b, c: (b, c, 0)),
                  pl.BlockSpec((1, SSM_CHUNK, conv_dim), lambda b, c: (b, c, 0)),
                  pl.BlockSpec((1, SSM_CHUNK, LANES), lambda b, c: (b, c, 0)),
                  pl.BlockSpec((SSM_CONV, conv_dim), fixed),
                  pl.BlockSpec((1, conv_dim), fixed),
                  pl.BlockSpec((1, LANES), fixed),
                  pl.BlockSpec((1, LANES), fixed),
                  pl.BlockSpec((1, d_inner), fixed),
                  pl.BlockSpec((1, d_inner), fixed)],
        out_specs=pl.BlockSpec((1, SSM_CHUNK, d_inner), lambda b, c: (b, c, 0)),
        out_shape=jax.ShapeDtypeStruct((bsz, t, d_inner), BF16),
        scratch_shapes=[pltpu.VMEM((SSM_CHUNK + 8, conv_dim), F32),
                        pltpu.VMEM((heads, SSM_HEADDIM, SSM_STATE), F32),
                        pltpu.VMEM((SSM_CHUNK, d_inner), F32)],
        compiler_params=_params(("parallel", "arbitrary")),
        name="ssd",
    )(z, xbc, dt, conv_w.reshape(SSM_CONV, conv_dim), conv_b.reshape(1, conv_dim), dtb, alog, dsk,
      norm_g.reshape(1, d_inner))


def _dilated_kernel(q_ref, kc_ref, kp_ref, vc_ref, vp_ref, b_ref, o_ref, l_ref):
    d = C_HEADDIM
    q = q_ref[0]
    k = jnp.concatenate([kp_ref[0], kc_ref[0]], axis=0)
    v = jnp.concatenate([vp_ref[0], vc_ref[0]], axis=0)
    for h in range(C_HEADS):
        sl = slice(h * d, (h + 1) * d)
        lg = lax.dot_general(q[:, sl], k[:, sl], (((1,), (1,)), ((), ())), preferred_element_type=F32)
        x = lg * d ** -0.5 + b_ref[0, h]
        m = jnp.max(x, axis=1, keepdims=True)
        p = jnp.exp(x - m)
        s = jnp.sum(p, axis=1, keepdims=True)
        o = jnp.dot(p.astype(BF16), v[:, sl], preferred_element_type=F32)
        o_ref[0, :, sl] = o * (1.0 / s)
        l_ref[0, :, sl] = jnp.broadcast_to(m + jnp.log(s), (Q_BLOCK, d))


def _dilated_group(qkv, bias, g, dilation):
    bsz, t, cols = qkv.shape
    hd = C_HEADS * C_HEADDIM
    per_tok = cols // hd
    tr = t // dilation
    view = qkv.reshape(bsz, tr, dilation * cols)
    col = lambda c, j: c * per_tok + g * 3 + j
    cur = lambda j: (lambda b, c, n: (b, n, col(c, j)))
    prev = lambda j: (lambda b, c, n: (b, jnp.maximum(n - 1, 0), col(c, j)))
    blk = (1, Q_BLOCK, hd)
    out_spec = pl.BlockSpec(blk, lambda b, c, n: (b, n, c))
    o, lse = pl.pallas_call(
        _dilated_kernel,
        grid=(bsz, dilation, tr // Q_BLOCK),
        in_specs=[pl.BlockSpec(blk, cur(0)), pl.BlockSpec(blk, cur(1)), pl.BlockSpec(blk, prev(1)),
                  pl.BlockSpec(blk, cur(2)), pl.BlockSpec(blk, prev(2)),
                  pl.BlockSpec((1, C_HEADS, Q_BLOCK, 2 * Q_BLOCK),
                               lambda b, c, n: (jnp.minimum(n, 1), 0, 0, 0))],
        out_specs=[out_spec, out_spec],
        out_shape=[jax.ShapeDtypeStruct((bsz, tr, dilation * hd), F32)] * 2,
        compiler_params=_params(("parallel", "parallel", "arbitrary")),
        name="dilated_attention",
    )(view, view, view, view, view, bias)
    return o.reshape(bsz * t, hd), lse.reshape(bsz * t, hd)


def _merge_kernel(x_ref, ya_ref, yb_ref, o1_ref, o2_ref, o3_ref, l1_ref, l2_ref, l3_ref,
                  ga_ref, gb_ref, gc_ref, wa_ref, wb_ref, wc_ref, wo_ref, out_ref):
    l1, l2, l3 = l1_ref[...], l2_ref[...], l3_ref[...]
    lm = jnp.maximum(jnp.maximum(l1, l2), l3)
    e1, e2, e3 = jnp.exp(l1 - lm), jnp.exp(l2 - lm), jnp.exp(l3 - lm)
    yc = (e1 * o1_ref[...] + e2 * o2_ref[...] + e3 * o3_ref[...]) * (1.0 / (e1 + e2 + e3))
    pa = jnp.dot(ya_ref[...], wa_ref[...], preferred_element_type=F32)
    pb = jnp.dot(yb_ref[...], wb_ref[...], preferred_element_type=F32)
    pc = jnp.dot(yc.astype(BF16), wc_ref[...], preferred_element_type=F32)
    merged = _sigmoid(ga_ref[...]) * pa + _sigmoid(gb_ref[...]) * pb + _sigmoid(gc_ref[...]) * pc
    out_ref[...] = x_ref[...] + jnp.dot(merged.astype(BF16), wo_ref[...], preferred_element_type=F32)


def _merge(x, ya, yb, oc, lc, gates, w_a, w_b, w_c, w_o, tm=256):
    m, d = x.shape
    row = lambda i: (i, 0)
    fixed = lambda i: (0, 0)
    full = lambda a: pl.BlockSpec(a.shape, fixed)
    rows = lambda a: pl.BlockSpec((tm, a.shape[1]), row)
    weights = [w_a.astype(BF16), w_b.astype(BF16), w_c.astype(BF16), w_o.astype(BF16)]
    return pl.pallas_call(
        _merge_kernel,
        grid=(m // tm,),
        in_specs=[rows(x), rows(ya), rows(yb)] + [rows(a) for a in oc] + [rows(a) for a in lc]
                 + [pl.BlockSpec((tm, d), lambda i, j=j: (i, j)) for j in range(3)]
                 + [full(w) for w in weights],
        out_specs=pl.BlockSpec((tm, d), row),
        out_shape=jax.ShapeDtypeStruct((m, d), F32),
        compiler_params=_params(("parallel",)),
        name="merge",
    )(x, ya, yb, *oc, *lc, gates, gates, gates, *weights)


def _pad_cols(w, n):
    return jnp.pad(w, ((0, 0), (0, n - w.shape[1])))


def kernel(x, p, rel_bias, final_norm, ffn1_norm, ffn1_gate, ffn1_up, ffn1_down, mix_norm, w_in,
           cq_norm, ckv_norm, w_uq, w_iq, conv_w, conv_b, dt_bias, a_log, d_skip, ssm_norm,
           w_out_a, w_out_b, w_out_c, w_o, ffn2_norm, ffn2_gate, ffn2_up, ffn2_down, ple_norm,
           w_ple_gate, w_ple_proj):
    bsz, t, d_model = x.shape
    depth = w_in.shape[0]
    m = bsz * t
    d_inner = ssm_norm.shape[1]
    conv_dim = conv_b.shape[1]
    ssm_heads = dt_bias.shape[1]
    c_cols = len(C_GROUPS) * 3 * C_HEADS * C_HEADDIM
    sizes = (A_DCQ, A_DC, IDX_DIM, IDX_HEADS, d_inner, conv_dim, ssm_heads, c_cols,
             d_model, d_model, d_model)
    offs = [0]
    for s in sizes:
        offs.append(offs[-1] + s)

    strips = _a_strips(rel_bias)
    c_bias = [_c_bias(rel_bias, dil, A_HEADS + g * C_HEADS) for g, (_, dil) in enumerate(C_GROUPS)]

    x = x.reshape(m, d_model)
    for i in range(depth):
        x = _ffn(x, ffn1_norm[i], ffn1_gate[i], ffn1_up[i], ffn1_down[i])

        w = w_in[i].astype(BF16)
        g_mix = mix_norm[i]
        a_in = _norm_matmul(x, g_mix, _pad_cols(w[:, offs[0]:offs[4]], 4 * LANES), F32, 512, 4 * LANES)
        z = _norm_matmul(x, g_mix, w[:, offs[4]:offs[5]], F32, 512, 1024)
        xbc = _norm_matmul(x, g_mix, w[:, offs[5]:offs[6]], F32, 512, 1024)
        dt = _norm_matmul(x, g_mix, _pad_cols(w[:, offs[6]:offs[7]], LANES), F32, 512, LANES)
        qkv = _norm_matmul(x, g_mix, w[:, offs[7]:offs[8]], BF16, 512, 1536)
        gates = _norm_matmul(x, g_mix, w[:, offs[8]:offs[11]], F32, 512, 1024)

        ya = _dsa_mixer(a_in, cq_norm[i], ckv_norm[i], w_uq[i], w_iq[i], strips, bsz, t)
        yb = _ssd_mixer(z.reshape(bsz, t, d_inner), xbc.reshape(bsz, t, conv_dim),
                        dt.reshape(bsz, t, LANES), conv_w[i], conv_b[i], dt_bias[i], a_log[i],
                        d_skip[i], ssm_norm[i])
        qkv = qkv.reshape(bsz, t, c_cols)
        oc, lc = zip(*[_dilated_group(qkv, c_bias[g], g, dil) for g, (_, dil) in enumerate(C_GROUPS)])
        x = _merge(x, ya.reshape(m, -1), yb.reshape(m, -1), oc, lc, gates,
                   w_out_a[i], w_out_b[i], w_out_c[i], w_o[i])

        x = _ffn(x, ffn2_norm[i], ffn2_gate[i], ffn2_up[i], ffn2_down[i])
        x = _ple(x, p[i].reshape(m, -1), ple_norm[i], w_ple_gate[i], w_ple_proj[i], final_norm,
                 final_norm=(i == depth - 1))
    return x.reshape(bsz, t, d_model)
```

```python
import functools
import math

import jax
import jax.numpy as jnp
from jax import lax
from jax.experimental import pallas as pl
from jax.experimental.pallas import tpu as pltpu

F32 = jnp.float32
BF16 = jnp.bfloat16
I32 = jnp.int32

EPS = 1e-6
LANES = 128
Q_BLOCK = 128
VMEM_LIMIT = 56 * 1024 * 1024

A_HEADS = 16
A_DC = 128
A_DCQ = 256
IDX_HEADS = 8
IDX_DIM = 64
TOPK_MAX = 256
A_HEAD_GROUP = 4
A_KEY_TILE = 512
SSM_HEADDIM = 64
SSM_GROUPS = 8
SSM_STATE = 128
SSM_CONV = 4
SSM_CHUNK = 128
C_GROUPS = ((128, 1), (512, 4), (2048, 16))
C_HEADS = 8
C_HEADDIM = 64
REL_BUCKETS = 32
REL_MAX_DIST = 2048

LOG2E = 1.4426950408889634
NEG_BIG = -1e30
KEY_NEG_INF = -2139095041
INT_MIN = -2147483648


def _params(sem, vmem=VMEM_LIMIT):
    return pltpu.CompilerParams(dimension_semantics=sem, vmem_limit_bytes=vmem)


def _rms(x, g):
    return x * lax.rsqrt(jnp.mean(x * x, axis=-1, keepdims=True) + EPS) * g


def _sigmoid(x):
    return 1.0 / (1.0 + jnp.exp(-x))


def _rel_bucket(d):
    max_exact = REL_BUCKETS // 2
    d = jnp.maximum(d, 0)
    df = jnp.maximum(d, 1).astype(F32)
    large = max_exact + (jnp.log(df / max_exact) / math.log(REL_MAX_DIST / max_exact)
                         * (REL_BUCKETS - max_exact)).astype(I32)
    large = jnp.minimum(large, REL_BUCKETS - 1)
    return jnp.where(d < max_exact, d, large)


def _norm_matmul_kernel(x_ref, g_ref, w_ref, o_ref, hn_ref):
    @pl.when(pl.program_id(1) == 0)
    def _():
        hn_ref[...] = _rms(x_ref[...], g_ref[...]).astype(BF16)

    o_ref[...] = jnp.dot(hn_ref[...], w_ref[...], preferred_element_type=F32).astype(o_ref.dtype)


def _norm_matmul(x, g, w, out_dtype, tm, tn):
    m, k = x.shape
    n = w.shape[1]
    return pl.pallas_call(
        _norm_matmul_kernel,
        grid=(m // tm, n // tn),
        in_specs=[pl.BlockSpec((tm, k), lambda i, j: (i, 0)),
                  pl.BlockSpec((1, k), lambda i, j: (0, 0)),
                  pl.BlockSpec((k, tn), lambda i, j: (0, j))],
        out_specs=pl.BlockSpec((tm, tn), lambda i, j: (i, j)),
        out_shape=jax.ShapeDtypeStruct((m, n), out_dtype),
        scratch_shapes=[pltpu.VMEM((tm, k), BF16)],
        compiler_params=_params(("parallel", "arbitrary")),
        name="norm_matmul",
    )(x, g.reshape(1, k), w)


def _ffn_kernel(x_ref, g_ref, wg_ref, wu_ref, wd_ref, o_ref, hn_ref, acc_ref):
    j = pl.program_id(1)

    @pl.when(j == 0)
    def _():
        hn_ref[...] = _rms(x_ref[...], g_ref[...]).astype(BF16)
        acc_ref[...] = jnp.zeros_like(acc_ref)

    h = hn_ref[...]
    gate = jnp.dot(h, wg_ref[...], preferred_element_type=F32)
    up = jnp.dot(h, wu_ref[...], preferred_element_type=F32)
    act = (gate * _sigmoid(gate) * up).astype(BF16)
    acc_ref[...] += jnp.dot(act, wd_ref[...], preferred_element_type=F32)

    @pl.when(j == pl.num_programs(1) - 1)
    def _():
        o_ref[...] = x_ref[...] + 0.5 * acc_ref[...]


def _ffn(x, g, w_gate, w_up, w_down, tm=512):
    m, d = x.shape
    f = w_gate.shape[1]
    tf = f // 2 if (f // 2) % LANES == 0 else f
    return pl.pallas_call(
        _ffn_kernel,
        grid=(m // tm, f // tf),
        in_specs=[pl.BlockSpec((tm, d), lambda i, j: (i, 0)),
                  pl.BlockSpec((1, d), lambda i, j: (0, 0)),
                  pl.BlockSpec((d, tf), lambda i, j: (0, j)),
                  pl.BlockSpec((d, tf), lambda i, j: (0, j)),
                  pl.BlockSpec((tf, d), lambda i, j: (j, 0))],
        out_specs=pl.BlockSpec((tm, d), lambda i, j: (i, 0)),
        out_shape=jax.ShapeDtypeStruct((m, d), F32),
        scratch_shapes=[pltpu.VMEM((tm, d), BF16), pltpu.VMEM((tm, d), F32)],
        compiler_params=_params(("parallel", "arbitrary")),
        name="ffn",
    )(x, g.reshape(1, d), w_gate.astype(BF16), w_up.astype(BF16), w_down.astype(BF16))


def _ple_kernel(x_ref, p_ref, g_ref, wg_ref, wp_ref, fg_ref, o_ref, *, final_norm):
    x = x_ref[...]
    h = _rms(x, g_ref[...]).astype(BF16)
    gate = _sigmoid(jnp.dot(h, wg_ref[...], preferred_element_type=F32))
    proj = jnp.dot(p_ref[...].astype(BF16), wp_ref[...], preferred_element_type=F32)
    y = x + gate * proj
    if final_norm:
        y = _rms(y, fg_ref[...])
    o_ref[...] = y


def _ple(x, p, g, w_gate, w_proj, final_g, final_norm, tm=512):
    m, d = x.shape
    dp = p.shape[1]
    return pl.pallas_call(
        functools.partial(_ple_kernel, final_norm=final_norm),
        grid=(m // tm,),
        in_specs=[pl.BlockSpec((tm, d), lambda i: (i, 0)),
                  pl.BlockSpec((tm, dp), lambda i: (i, 0)),
                  pl.BlockSpec((1, d), lambda i: (0, 0)),
                  pl.BlockSpec((d, d), lambda i: (0, 0)),
                  pl.BlockSpec((dp, d), lambda i: (0, 0)),
                  pl.BlockSpec((1, d), lambda i: (0, 0))],
        out_specs=pl.BlockSpec((tm, d), lambda i: (i, 0)),
        out_shape=jax.ShapeDtypeStruct((m, d), F32),
        compiler_params=_params(("parallel",)),
        name="ple",
    )(x, p, g.reshape(1, d), w_gate.astype(BF16), w_proj.astype(BF16), final_g.reshape(1, d))


def _dsa_prep_kernel(a_ref, cqg_ref, ckvg_ref, wq_ref, ql_ref, qi_ref, kvt_ref, kva_ref, kit_ref,
                     wit_ref):
    a = a_ref[0]
    tm = a.shape[0]
    cqn = _rms(a[:, :A_DCQ], cqg_ref[...]).astype(BF16)
    q = jnp.dot(cqn, wq_ref[...], preferred_element_type=F32)
    nl = A_HEADS * A_DC
    ql = (q[:, :nl] * (A_DC ** -0.5 * LOG2E)).astype(BF16)
    for h in range(A_HEADS):
        ql_ref[0, h] = ql[:, h * A_DC:(h + 1) * A_DC]
    qi = q[:, nl:].astype(BF16)
    for h in range(IDX_HEADS):
        qi_ref[0, h] = qi[:, h * IDX_DIM:(h + 1) * IDX_DIM]
    kv = _rms(a[:, A_DCQ:A_DCQ + A_DC], ckvg_ref[...])
    kvt_ref[0] = kv.T.astype(BF16)
    lane = lax.broadcasted_iota(I32, (tm, A_DC), 1)
    kva_ref[0, :, :A_DC] = kv.astype(BF16)
    kva_ref[0, :, A_DC:] = jnp.where(lane == 0, 1.0, 0.0).astype(BF16)
    o = A_DCQ + A_DC
    rest_t = a[:, o:o + LANES].T
    kit_ref[0] = rest_t[:IDX_DIM].astype(BF16)
    wit_ref[0] = rest_t[IDX_DIM:IDX_DIM + IDX_HEADS] * (IDX_HEADS * IDX_DIM) ** -0.5


def _dsa_prep(a_in, cq_g, ckv_g, w_q, tm=512):
    bsz, t, na = a_in.shape
    nq = w_q.shape[1]
    fixed = lambda b, i: (0, 0)
    return pl.pallas_call(
        _dsa_prep_kernel,
        grid=(bsz, t // tm),
        in_specs=[pl.BlockSpec((1, tm, na), lambda b, i: (b, i, 0)),
                  pl.BlockSpec((1, A_DCQ), fixed),
                  pl.BlockSpec((1, A_DC), fixed),
                  pl.BlockSpec((A_DCQ, nq), fixed)],
        out_specs=[pl.BlockSpec((1, A_HEADS, tm, A_DC), lambda b, i: (b, 0, i, 0)),
                   pl.BlockSpec((1, IDX_HEADS, tm, IDX_DIM), lambda b, i: (b, 0, i, 0)),
                   pl.BlockSpec((1, A_DC, tm), lambda b, i: (b, 0, i)),
                   pl.BlockSpec((1, tm, 2 * A_DC), lambda b, i: (b, i, 0)),
                   pl.BlockSpec((1, IDX_DIM, tm), lambda b, i: (b, 0, i)),
                   pl.BlockSpec((1, IDX_HEADS, tm), lambda b, i: (b, 0, i))],
        out_shape=[jax.ShapeDtypeStruct((bsz, A_HEADS, t, A_DC), BF16),
                   jax.ShapeDtypeStruct((bsz, IDX_HEADS, t, IDX_DIM), BF16),
                   jax.ShapeDtypeStruct((bsz, A_DC, t), BF16),
                   jax.ShapeDtypeStruct((bsz, t, 2 * A_DC), BF16),
                   jax.ShapeDtypeStruct((bsz, IDX_DIM, t), BF16),
                   jax.ShapeDtypeStruct((bsz, IDX_HEADS, t), F32)],
        compiler_params=_params(("parallel", "parallel")),
        name="dsa_prep",
    )(a_in, cq_g.reshape(1, -1), ckv_g.reshape(1, -1), w_q)


def _a_strip_off():
    return REL_MAX_DIST + A_KEY_TILE


def _a_strip_len():
    return _a_strip_off() + A_KEY_TILE


def _a_strip_kernel(tbl_ref, o_ref):
    h = pl.program_id(0)
    shape = (Q_BLOCK, _a_strip_len())
    d = (lax.broadcasted_iota(I32, shape, 0) + _a_strip_off()
         - lax.broadcasted_iota(I32, shape, 1))
    bucket = _rel_bucket(d)
    far = tbl_ref[REL_BUCKETS - 1, h]
    val = jnp.zeros(shape, F32)
    for b in range(REL_BUCKETS - 1):
        val = jnp.where(bucket == b, (tbl_ref[b, h] - far) * LOG2E, val)
    o_ref[0] = val


def _a_strips(tbl):
    return pl.pallas_call(
        _a_strip_kernel,
        grid=(A_HEADS,),
        in_specs=[pl.BlockSpec(memory_space=pltpu.SMEM)],
        out_specs=pl.BlockSpec((1, Q_BLOCK, _a_strip_len()), lambda h: (h, 0, 0)),
        out_shape=jax.ShapeDtypeStruct((A_HEADS, Q_BLOCK, _a_strip_len()), F32),
        compiler_params=_params(("arbitrary",)),
        name="dsa_bias_strips",
    )(tbl)


def _c_bias_kernel(tbl_ref, o_ref, *, dilation, col0):
    v = pl.program_id(0)
    h = pl.program_id(1)
    shape = (Q_BLOCK, 2 * Q_BLOCK)
    i = lax.broadcasted_iota(I32, shape, 0)
    j = lax.broadcasted_iota(I32, shape, 1)
    dist = Q_BLOCK + i - j
    bucket = _rel_bucket(dist * dilation)
    val = jnp.zeros(shape, F32)
    for b in range(REL_BUCKETS):
        val = jnp.where(bucket == b, tbl_ref[b, col0 + h], val)
    valid = (dist >= 0) & (dist <= Q_BLOCK) & ((j >= Q_BLOCK) | (v > 0))
    o_ref[0, 0] = jnp.where(valid, val, NEG_BIG)


def _c_bias(tbl, dilation, col0):
    return pl.pallas_call(
        functools.partial(_c_bias_kernel, dilation=dilation, col0=col0),
        grid=(2, C_HEADS),
        in_specs=[pl.BlockSpec(memory_space=pltpu.SMEM)],
        out_specs=pl.BlockSpec((1, 1, Q_BLOCK, 2 * Q_BLOCK), lambda v, h: (v, h, 0, 0)),
        out_shape=jax.ShapeDtypeStruct((2, C_HEADS, Q_BLOCK, 2 * Q_BLOCK), F32),
        compiler_params=_params(("arbitrary", "arbitrary")),
        name="dilated_bias",
    )(tbl)


def _dsa_kernel(qi_ref, wi_ref, kit_ref, q_ref, kvt_ref, kva_ref, strip_ref, o_ref,
                sc_ref, thr_ref, kaug_ref, qa_ref, m_ref, acc_ref, lg_ref, p_ref, al_ref, *, topk, seq):
    qb = pl.program_id(1)
    hg = pl.program_id(2)
    w_tile = A_KEY_TILE
    q0 = qb * Q_BLOCK
    n_tiles = q0 // w_tile + 1
    n_pairs = (n_tiles + 1) // 2
    cols = w_tile // LANES
    rows = A_HEAD_GROUP * Q_BLOCK

    @pl.when((qb == 0) & (hg == 0))
    def _load_keys():
        kaug_ref[:A_DC, :] = kvt_ref[0]
    lane_iota = lax.broadcasted_iota(I32, (Q_BLOCK, w_tile), 1)
    row_iota = lax.broadcasted_iota(I32, (Q_BLOCK, w_tile), 0)

    def tile_start(kt):
        return pl.multiple_of(kt * w_tile, w_tile)

    @pl.when(hg == 0)
    def _select():
        qi = qi_ref[0].reshape(IDX_HEADS * Q_BLOCK, IDX_DIM)
        wcols = [wi_ref[0, h] for h in range(IDX_HEADS)]

        def score_tile(kt, carry):
            k0 = tile_start(kt)
            r = jnp.dot(qi, kit_ref[0, :, pl.ds(k0, w_tile)], preferred_element_type=F32)
            r = jnp.maximum(r, 0.0).reshape(IDX_HEADS, Q_BLOCK, w_tile)
            s = r[0] * wcols[0]
            for h in range(1, IDX_HEADS):
                s = s + r[h] * wcols[h]
            s = jnp.where(k0 + lane_iota <= q0 + row_iota, s, -jnp.inf)
            bits = pltpu.bitcast(s, I32)
            sc_ref[:, pl.ds(k0, w_tile)] = bits ^ ((bits >> 31) & 0x7FFFFFFF)
            return carry

        lax.fori_loop(0, n_tiles, score_tile, 0)

        def count(pred):
            def body(kt, cnt):
                k0 = tile_start(kt)
                blk = sc_ref[:, pl.ds(k0, w_tile)]
                for c in range(cols):
                    hit = pred(blk[:, c * LANES:(c + 1) * LANES], k0 + c * LANES)
                    cnt = cnt + jnp.where(hit, 1, 0)
                return cnt
            cnt = lax.fori_loop(0, n_tiles, body, jnp.zeros((Q_BLOCK, LANES), I32))
            return jnp.sum(cnt, axis=1, keepdims=True)

        def bcast(v):
            return jnp.broadcast_to(v, (Q_BLOCK, LANES))

        def kth_bit(i, t):
            cand = t | lax.shift_left(jnp.int32(1), 31 - i)
            cand_b = bcast(cand ^ INT_MIN)
            c = count(lambda blk, _: blk >= cand_b)
            return jnp.where(c >= topk, cand, t)

        t = lax.fori_loop(0, 32, kth_bit, jnp.zeros((Q_BLOCK, 1), I32))
        thr = jnp.maximum(t ^ INT_MIN, KEY_NEG_INF + 1)
        thr_b = bcast(thr)
        thr_ref[:, 0:1] = thr
        n_gt = count(lambda blk, _: blk > thr_b)
        n_eq = count(lambda blk, _: blk == thr_b)
        need = topk - n_gt
        thr_ref[:, 1:2] = jnp.full((Q_BLOCK, 1), 2 * seq, I32)

        @pl.when(jnp.max(n_eq - need) > 0)
        def _ties():
            lane = lax.broadcasted_iota(I32, (Q_BLOCK, LANES), 1)
            nbits = (2 * seq - 1).bit_length()

            def lim_bit(i, lim):
                cand = lim | lax.shift_left(jnp.int32(1), nbits - 1 - i)
                cand_b = bcast(cand)
                c = count(lambda blk, k0: jnp.where(blk == thr_b, k0 + lane, 2 * seq) < cand_b)
                return jnp.where(c <= need, cand, lim)

            thr_ref[:, 1:2] = lax.fori_loop(0, nbits, lim_bit, jnp.zeros((Q_BLOCK, 1), I32))

        lim = thr_ref[:, 1:2]

        def mask_tile(kt, carry):
            k0 = tile_start(kt)
            blk = sc_ref[:, pl.ds(k0, w_tile)]
            bound = jnp.where(k0 + lane_iota < lim, thr, thr + 1)
            add = jnp.where(blk >= bound, 0.0, NEG_BIG).astype(F32)
            add = jnp.where(kt < n_tiles, add, NEG_BIG)
            kaug_ref[A_DC:, pl.ds(k0, w_tile)] = add.astype(BF16)
            return carry

        lax.fori_loop(0, 2 * n_pairs, mask_tile, 0)

    chunk = 32
    eye = (lax.broadcasted_iota(I32, (Q_BLOCK, Q_BLOCK), 0)
           == lax.broadcasted_iota(I32, (Q_BLOCK, Q_BLOCK), 1))
    onehot = jnp.where(eye, 1.0, 0.0).astype(BF16)
    for hh in range(A_HEAD_GROUP):
        qa_ref[hh * Q_BLOCK:(hh + 1) * Q_BLOCK, :A_DC] = q_ref[0, hh]
        qa_ref[hh * Q_BLOCK:(hh + 1) * Q_BLOCK, A_DC:] = onehot
    m_ref[...] = jnp.full(m_ref.shape, NEG_BIG, F32)
    acc_ref[...] = jnp.zeros_like(acc_ref)

    def stage_a(kt, slot):
        k0 = tile_start(kt)
        lg_ref[slot] = jnp.dot(qa_ref[...], kaug_ref[:, pl.ds(k0, w_tile)], preferred_element_type=F32)

    def stage_b(kt, slot, near):
        k0 = tile_start(kt)
        l0 = pl.multiple_of(jnp.clip(_a_strip_off() - (q0 - k0), 0, _a_strip_off()), LANES)
        for hh in range(A_HEAD_GROUP):
            for r0 in range(0, Q_BLOCK, chunk):
                rs = pl.ds(hh * Q_BLOCK + r0, chunk)
                x = lg_ref[slot, rs, :]
                if near:
                    x = x + strip_ref[hh, pl.ds(r0, chunk), pl.ds(l0, w_tile)]
                m_old = m_ref[rs, :]
                m_new = jnp.maximum(m_old, jnp.max(x, axis=1, keepdims=True))
                for c in range(cols):
                    cs = slice(c * LANES, (c + 1) * LANES)
                    p_ref[slot, rs, cs] = jnp.exp2(x[:, cs] - m_new).astype(BF16)
                al_ref[slot, rs, :] = jnp.exp2(m_old - m_new)
                m_ref[rs, :] = m_new

    def stage_c(kt, slot):
        k0 = tile_start(kt)
        pv = jnp.dot(p_ref[slot], kva_ref[0, pl.ds(k0, w_tile), :], preferred_element_type=F32)
        al = al_ref[slot]
        acc_ref[:, :A_DC] = acc_ref[:, :A_DC] * al + pv[:, :A_DC]
        acc_ref[:, A_DC:] = acc_ref[:, A_DC:] * al + pv[:, A_DC:]

    stage_a(0, 0)
    stage_a(1, 1)
    stage_b(0, 0, True)

    def pair_body(i, near):
        kt = 2 * i
        stage_a(kt, 0)
        stage_b(kt - 1, 1, near)
        stage_c(kt - 2, 0)
        stage_a(kt + 1, 1)
        stage_b(kt, 0, near)
        stage_c(kt - 1, 1)

    def run_pairs(lo, hi, near):
        def two_pairs(j, carry):
            pair_body(lo + 2 * j, near)
            pair_body(lo + 2 * j + 1, near)
            return carry

        lax.fori_loop(0, (hi - lo) // 2, two_pairs, 0)

        @pl.when((hi - lo) % 2 == 1)
        def _():
            pair_body(hi - 1, near)

    n_far = jnp.maximum(q0 - (REL_MAX_DIST - 1), 0) // w_tile
    first_near = jnp.maximum((n_far + 1) // 2, 1)
    run_pairs(1, first_near, False)
    run_pairs(first_near, n_pairs, True)
    last = 2 * n_pairs - 1
    stage_b(last, 1, True)
    stage_c(last - 1, 0)
    stage_c(last, 1)

    acc = acc_ref[...]
    out = acc[:, :A_DC] * (1.0 / acc[:, A_DC:A_DC + 1])
    for hh in range(A_HEAD_GROUP):
        o_ref[0, :, hh * A_DC:(hh + 1) * A_DC] = out[hh * Q_BLOCK:(hh + 1) * Q_BLOCK].astype(o_ref.dtype)


def _dsa_attention(qi, wi, kit, q, kvt, kva, strips, topk):
    bsz, _, t, _ = q.shape
    assert t % (2 * A_KEY_TILE) == 0
    groups = A_HEADS // A_HEAD_GROUP
    rows = A_HEAD_GROUP * Q_BLOCK
    resident = dict(pipeline_mode=pl.Buffered(1))
    return pl.pallas_call(
        functools.partial(_dsa_kernel, topk=topk, seq=t),
        grid=(bsz, t // Q_BLOCK, groups),
        in_specs=[
            pl.BlockSpec((1, IDX_HEADS, Q_BLOCK, IDX_DIM), lambda b, i, g: (b, 0, i, 0)),
            pl.BlockSpec((1, IDX_HEADS, Q_BLOCK, 1), lambda b, i, g: (b, 0, i, 0)),
            pl.BlockSpec((1, IDX_DIM, t), lambda b, i, g: (b, 0, 0), **resident),
            pl.BlockSpec((1, A_HEAD_GROUP, Q_BLOCK, A_DC), lambda b, i, g: (b, g, i, 0)),
            pl.BlockSpec((1, A_DC, t), lambda b, i, g: (b, 0, 0), **resident),
            pl.BlockSpec((1, t, 2 * A_DC), lambda b, i, g: (b, 0, 0), **resident),
            pl.BlockSpec((A_HEAD_GROUP, Q_BLOCK, _a_strip_len()), lambda b, i, g: (g, 0, 0)),
        ],
        out_specs=pl.BlockSpec((1, Q_BLOCK, A_HEAD_GROUP * A_DC), lambda b, i, g: (b, i, g)),
        out_shape=jax.ShapeDtypeStruct((bsz, t, A_HEADS * A_DC), BF16),
        scratch_shapes=[pltpu.VMEM((Q_BLOCK, t), I32),
                        pltpu.VMEM((Q_BLOCK, LANES), I32),
                        pltpu.VMEM((2 * A_DC, t), BF16),
                        pltpu.VMEM((rows, 2 * A_DC), BF16),
                        pltpu.VMEM((rows, LANES), F32),
                        pltpu.VMEM((rows, 2 * A_DC), F32),
                        pltpu.VMEM((2, rows, A_KEY_TILE), F32),
                        pltpu.VMEM((2, rows, A_KEY_TILE), BF16),
                        pltpu.VMEM((2, rows, LANES), F32)],
        compiler_params=_params(("parallel", "arbitrary", "arbitrary")),
        name="dsa_attention",
    )(qi, wi, kit, q, kvt, kva, strips)


def _dsa_mixer(a_in, cq_g, ckv_g, w_uq, w_iq, strips, bsz, t):
    w_q = jnp.concatenate([w_uq, w_iq], axis=1).astype(BF16)
    ql, qi, kvt, kva, kit, wit = _dsa_prep(a_in.reshape(bsz, t, -1), cq_g, ckv_g, w_q)
    topk = min(TOPK_MAX, t // 4)
    return _dsa_attention(qi, wit[..., None], kit, ql, kvt, kva, strips, topk)


def _split3(x):
    hi = x.astype(BF16)
    r1 = x - hi.astype(F32)
    mid = r1.astype(BF16)
    lo = (r1 - mid.astype(F32)).astype(BF16)
    return hi, mid, lo


def _ssd_kernel(z_ref, xbc_ref, dt_ref, cw_ref, cb_ref, dtb_ref, alog_ref, dsk_ref, ng_ref, y_ref,
                ext_ref, st_ref, yb_ref, *, d_inner):
    c = pl.program_id(1)
    q = SSM_CHUNK
    n_state = SSM_STATE
    p_dim = SSM_HEADDIM
    hpg = d_inner // p_dim // SSM_GROUPS
    hist = 8

    @pl.when(c == 0)
    def _():
        ext_ref[0:hist, :] = jnp.zeros((hist, ext_ref.shape[1]), F32)
        st_ref[...] = jnp.zeros_like(st_ref)

    ext_ref[hist:hist + q, :] = xbc_ref[0]
    conv = cb_ref[...]
    for k in range(SSM_CONV):
        lo = hist - (SSM_CONV - 1) + k
        conv = conv + cw_ref[k:k + 1, :] * ext_ref[lo:lo + q, :]
    ext_ref[0:hist, :] = ext_ref[q:q + hist, :]
    u = conv * _sigmoid(conv)
    gn = SSM_GROUPS * n_state
    xs = u[:, :d_inner]
    bm = u[:, d_inner:d_inner + gn]
    cm = u[:, d_inner + gn:]

    raw = dt_ref[0] + dtb_ref[...]
    dt = jnp.maximum(raw, 0.0) + jnp.log(1.0 + jnp.exp(-jnp.abs(raw)))
    dta = dt * (-jnp.exp(alog_ref[...]))
    ii = lax.broadcasted_iota(I32, (q, q), 0)
    jj = lax.broadcasted_iota(I32, (q, q), 1)
    tril = ii >= jj
    tri = jnp.where(tril, 1.0, 0.0).astype(BF16)
    cum = sum(jnp.dot(tri, part, preferred_element_type=F32) for part in _split3(dta))
    cum_t = cum.T
    dt_t = dt.T
    last_t = cum_t[:, q - 1:q]
    ecum = jnp.exp(cum)
    wst_t = jnp.exp(last_t - cum_t) * dt_t
    elast = jnp.exp(last_t)

    for g in range(SSM_GROUPS):
        bc = bm[:, g * n_state:(g + 1) * n_state]
        cc = cm[:, g * n_state:(g + 1) * n_state]
        bc16 = bc.astype(BF16)
        cc16 = cc.astype(BF16)
        cb = lax.dot_general(cc16, bc16, (((1,), (1,)), ((), ())), preferred_element_type=F32)
        xs_t = xs[:, g * hpg * p_dim:(g + 1) * hpg * p_dim].T
        for hh in range(hpg):
            h = g * hpg + hh
            seg = cum[:, h:h + 1] - cum_t[h:h + 1, :]
            decay = jnp.exp(jnp.where(tril, seg, -jnp.inf))
            wgt = (decay * cb * dt_t[h:h + 1, :]).astype(BF16)
            xc16 = xs[:, h * p_dim:(h + 1) * p_dim].astype(BF16)
            st = st_ref[h]
            y_off = lax.dot_general(cc16, st.astype(BF16), (((1,), (1,)), ((), ())),
                                    preferred_element_type=F32)
            y = jnp.dot(wgt, xc16, preferred_element_type=F32) + y_off * ecum[:, h:h + 1]
            wx_t = (xs_t[hh * p_dim:(hh + 1) * p_dim, :] * wst_t[h:h + 1, :]).astype(BF16)
            st_ref[h] = st * elast[h:h + 1, :] + jnp.dot(wx_t, bc16, preferred_element_type=F32)
            yb_ref[:, h * p_dim:(h + 1) * p_dim] = y

    z = z_ref[0]
    y = (yb_ref[...] + xs * dsk_ref[...]) * (z * _sigmoid(z))
    gw = hpg * p_dim
    for g in range(SSM_GROUPS):
        yg = y[:, g * gw:(g + 1) * gw]
        yn = yg * lax.rsqrt(jnp.mean(yg * yg, axis=-1, keepdims=True) + EPS)
        y_ref[0, :, g * gw:(g + 1) * gw] = (yn * ng_ref[:, g * gw:(g + 1) * gw]).astype(y_ref.dtype)


def _ssd_mixer(z, xbc, dt, conv_w, conv_b, dt_bias, a_log, d_skip, norm_g):
    bsz, t, d_inner = z.shape
    conv_dim = xbc.shape[-1]
    heads = d_inner // SSM_HEADDIM
    pad = LANES - heads
    dtb = jnp.pad(dt_bias, (0, pad)).reshape(1, LANES)
    alog = jnp.pad(a_log, (0, pad)).reshape(1, LANES)
    dsk = jnp.repeat(d_skip, SSM_HEADDIM).reshape(1, d_inner)
    fixed = lambda b, c: (0, 0)
    return pl.pallas_call(
        functools.partial(_ssd_kernel, d_inner=d_inner),
        grid=(bsz, t // SSM_CHUNK),
        in_specs=[pl.BlockSpec((1, SSM_CHUNK, d_inner), lambda b, c: (b, c, 0)),
                  pl.BlockSpec((1, SSM_CHUNK, conv_dim), lambda b, c: (b, c, 0)),
                  pl.BlockSpec((1, SSM_CHUNK, LANES), lambda b, c: (b, c, 0)),
                  pl.BlockSpec((SSM_CONV, conv_dim), fixed),
                  pl.BlockSpec((1, conv_dim), fixed),
                  pl.BlockSpec((1, LANES), fixed),
                  pl.BlockSpec((1, LANES), fixed),
                  pl.BlockSpec((1, d_inner), fixed),
                  pl.BlockSpec((1, d_inner), fixed)],
        out_specs=pl.BlockSpec((1, SSM_CHUNK, d_inner), lambda b, c: (b, c, 0)),
        out_shape=jax.ShapeDtypeStruct((bsz, t, d_inner), BF16),
        scratch_shapes=[pltpu.VMEM((SSM_CHUNK + 8, conv_dim), F32),
                        pltpu.VMEM((heads, SSM_HEADDIM, SSM_STATE), F32),
                        pltpu.VMEM((SSM_CHUNK, d_inner), F32)],
        compiler_params=_params(("parallel", "arbitrary")),
        name="ssd",
    )(z, xbc, dt, conv_w.reshape(SSM_CONV, conv_dim), conv_b.reshape(1, conv_dim), dtb, alog, dsk,
      norm_g.reshape(1, d_inner))


def _dilated_kernel(q_ref, kc_ref, kp_ref, vc_ref, vp_ref, b_ref, o_ref, l_ref):
    d = C_HEADDIM
    q = q_ref[0]
    k = jnp.concatenate([kp_ref[0], kc_ref[0]], axis=0)
    v = jnp.concatenate([vp_ref[0], vc_ref[0]], axis=0)
    for h in range(C_HEADS):
        sl = slice(h * d, (h + 1) * d)
        lg = lax.dot_general(q[:, sl], k[:, sl], (((1,), (1,)), ((), ())), preferred_element_type=F32)
        x = lg * d ** -0.5 + b_ref[0, h]
        m = jnp.max(x, axis=1, keepdims=True)
        p = jnp.exp(x - m)
        s = jnp.sum(p, axis=1, keepdims=True)
        o = jnp.dot(p.astype(BF16), v[:, sl], preferred_element_type=F32)
        o_ref[0, :, sl] = o * (1.0 / s)
        l_ref[0, :, sl] = jnp.broadcast_to(m + jnp.log(s), (Q_BLOCK, d))


def _dilated_group(qkv, bias, g, dilation):
    bsz, t, cols = qkv.shape
    hd = C_HEADS * C_HEADDIM
    per_tok = cols // hd
    tr = t // dilation
    view = qkv.reshape(bsz, tr, dilation * cols)
    col = lambda c, j: c * per_tok + g * 3 + j
    cur = lambda j: (lambda b, c, n: (b, n, col(c, j)))
    prev = lambda j: (lambda b, c, n: (b, jnp.maximum(n - 1, 0), col(c, j)))
    blk = (1, Q_BLOCK, hd)
    out_spec = pl.BlockSpec(blk, lambda b, c, n: (b, n, c))
    o, lse = pl.pallas_call(
        _dilated_kernel,
        grid=(bsz, dilation, tr // Q_BLOCK),
        in_specs=[pl.BlockSpec(blk, cur(0)), pl.BlockSpec(blk, cur(1)), pl.BlockSpec(blk, prev(1)),
                  pl.BlockSpec(blk, cur(2)), pl.BlockSpec(blk, prev(2)),
                  pl.BlockSpec((1, C_HEADS, Q_BLOCK, 2 * Q_BLOCK),
                               lambda b, c, n: (jnp.minimum(n, 1), 0, 0, 0))],
        out_specs=[out_spec, out_spec],
        out_shape=[jax.ShapeDtypeStruct((bsz, tr, dilation * hd), F32)] * 2,
        compiler_params=_params(("parallel", "parallel", "arbitrary")),
        name="dilated_attention",
    )(view, view, view, view, view, bias)
    return o.reshape(bsz * t, hd), lse.reshape(bsz * t, hd)


def _merge_kernel(x_ref, ya_ref, yb_ref, o1_ref, o2_ref, o3_ref, l1_ref, l2_ref, l3_ref,
                  ga_ref, gb_ref, gc_ref, wa_ref, wb_ref, wc_ref, wo_ref, out_ref):
    l1, l2, l3 = l1_ref[...], l2_ref[...], l3_ref[...]
    lm = jnp.maximum(jnp.maximum(l1, l2), l3)
    e1, e2, e3 = jnp.exp(l1 - lm), jnp.exp(l2 - lm), jnp.exp(l3 - lm)
    yc = (e1 * o1_ref[...] + e2 * o2_ref[...] + e3 * o3_ref[...]) * (1.0 / (e1 + e2 + e3))
    pa = jnp.dot(ya_ref[...], wa_ref[...], preferred_element_type=F32)
    pb = jnp.dot(yb_ref[...], wb_ref[...], preferred_element_type=F32)
    pc = jnp.dot(yc.astype(BF16), wc_ref[...], preferred_element_type=F32)
    merged = _sigmoid(ga_ref[...]) * pa + _sigmoid(gb_ref[...]) * pb + _sigmoid(gc_ref[...]) * pc
    out_ref[...] = x_ref[...] + jnp.dot(merged.astype(BF16), wo_ref[...], preferred_element_type=F32)


def _merge(x, ya, yb, oc, lc, gates, w_a, w_b, w_c, w_o, tm=256):
    m, d = x.shape
    row = lambda i: (i, 0)
    fixed = lambda i: (0, 0)
    full = lambda a: pl.BlockSpec(a.shape, fixed)
    rows = lambda a: pl.BlockSpec((tm, a.shape[1]), row)
    weights = [w_a.astype(BF16), w_b.astype(BF16), w_c.astype(BF16), w_o.astype(BF16)]
    return pl.pallas_call(
        _merge_kernel,
        grid=(m // tm,),
        in_specs=[rows(x), rows(ya), rows(yb)] + [rows(a) for a in oc] + [rows(a) for a in lc]
                 + [pl.BlockSpec((tm, d), lambda i, j=j: (i, j)) for j in range(3)]
                 + [full(w) for w in weights],
        out_specs=pl.BlockSpec((tm, d), row),
        out_shape=jax.ShapeDtypeStruct((m, d), F32),
        compiler_params=_params(("parallel",)),
        name="merge",
    )(x, ya, yb, *oc, *lc, gates, gates, gates, *weights)


def _pad_cols(w, n):
    return jnp.pad(w, ((0, 0), (0, n - w.shape[1])))


def kernel(x, p, rel_bias, final_norm, ffn1_norm, ffn1_gate, ffn1_up, ffn1_down, mix_norm, w_in,
           cq_norm, ckv_norm, w_uq, w_iq, conv_w, conv_b, dt_bias, a_log, d_skip, ssm_norm,
           w_out_a, w_out_b, w_out_c, w_o, ffn2_norm, ffn2_gate, ffn2_up, ffn2_down, ple_norm,
           w_ple_gate, w_ple_proj):
    bsz, t, d_model = x.shape
    depth = w_in.shape[0]
    m = bsz * t
    d_inner = ssm_norm.shape[1]
    conv_dim = conv_b.shape[1]
    ssm_heads = dt_bias.shape[1]
    c_cols = len(C_GROUPS) * 3 * C_HEADS * C_HEADDIM
    sizes = (A_DCQ, A_DC, IDX_DIM, IDX_HEADS, d_inner, conv_dim, ssm_heads, c_cols,
             d_model, d_model, d_model)
    offs = [0]
    for s in sizes:
        offs.append(offs[-1] + s)

    strips = _a_strips(rel_bias)
    c_bias = [_c_bias(rel_bias, dil, A_HEADS + g * C_HEADS) for g, (_, dil) in enumerate(C_GROUPS)]

    x = x.reshape(m, d_model)
    for i in range(depth):
        x = _ffn(x, ffn1_norm[i], ffn1_gate[i], ffn1_up[i], ffn1_down[i])

        w = w_in[i].astype(BF16)
        g_mix = mix_norm[i]
        a_in = _norm_matmul(x, g_mix, _pad_cols(w[:, offs[0]:offs[4]], 4 * LANES), F32, 512, 4 * LANES)
        z = _norm_matmul(x, g_mix, w[:, offs[4]:offs[5]], F32, 1024, 1024)
        xbc = _norm_matmul(x, g_mix, w[:, offs[5]:offs[6]], F32, 1024, 1024)
        dt = _norm_matmul(x, g_mix, _pad_cols(w[:, offs[6]:offs[7]], LANES), F32, 512, LANES)
        qkv = _norm_matmul(x, g_mix, w[:, offs[7]:offs[8]], BF16, 1024, 1536)
        gates = _norm_matmul(x, g_mix, w[:, offs[8]:offs[11]], F32, 1024, 1024)

        ya = _dsa_mixer(a_in, cq_norm[i], ckv_norm[i], w_uq[i], w_iq[i], strips, bsz, t)
        yb = _ssd_mixer(z.reshape(bsz, t, d_inner), xbc.reshape(bsz, t, conv_dim),
                        dt.reshape(bsz, t, LANES), conv_w[i], conv_b[i], dt_bias[i], a_log[i],
                        d_skip[i], ssm_norm[i])
        qkv = qkv.reshape(bsz, t, c_cols)
        oc, lc = zip(*[_dilated_group(qkv, c_bias[g], g, dil) for g, (_, dil) in enumerate(C_GROUPS)])
        x = _merge(x, ya.reshape(m, -1), yb.reshape(m, -1), oc, lc, gates,
                   w_out_a[i], w_out_b[i], w_out_c[i], w_o[i])

        x = _ffn(x, ffn2_norm[i], ffn2_gate[i], ffn2_up[i], ffn2_down[i])
        x = _ple(x, p[i].reshape(m, -1), ple_norm[i], w_ple_gate[i], w_ple_proj[i], final_norm,
                 final_norm=(i == depth - 1))
    return x.reshape(bsz, t, d_model)
```

```python
import functools
import math

import jax
import jax.numpy as jnp
from jax import lax
from jax.experimental import pallas as pl
from jax.experimental.pallas import tpu as pltpu

F32 = jnp.float32
BF16 = jnp.bfloat16
I32 = jnp.int32

EPS = 1e-6
LANES = 128
Q_BLOCK = 128
VMEM_LIMIT = 56 * 1024 * 1024

A_HEADS = 16
A_DC = 128
A_DCQ = 256
IDX_HEADS = 8
IDX_DIM = 64
TOPK_MAX = 256
A_HEAD_GROUP = 4
A_KEY_TILE = 512
SSM_HEADDIM = 64
SSM_GROUPS = 8
SSM_STATE = 128
SSM_CONV = 4
SSM_CHUNK = 128
C_GROUPS = ((128, 1), (512, 4), (2048, 16))
C_HEADS = 8
C_HEADDIM = 64
REL_BUCKETS = 32
REL_MAX_DIST = 2048

LOG2E = 1.4426950408889634
NEG_BIG = -1e30
KEY_NEG_INF = -2139095041
INT_MIN = -2147483648


def _params(sem, vmem=VMEM_LIMIT):
    return pltpu.CompilerParams(dimension_semantics=sem, vmem_limit_bytes=vmem)


def _rms(x, g):
    return x * lax.rsqrt(jnp.mean(x * x, axis=-1, keepdims=True) + EPS) * g


def _sigmoid(x):
    return 1.0 / (1.0 + jnp.exp(-x))


def _rel_bucket(d):
    max_exact = REL_BUCKETS // 2
    d = jnp.maximum(d, 0)
    df = jnp.maximum(d, 1).astype(F32)
    large = max_exact + (jnp.log(df / max_exact) / math.log(REL_MAX_DIST / max_exact)
                         * (REL_BUCKETS - max_exact)).astype(I32)
    large = jnp.minimum(large, REL_BUCKETS - 1)
    return jnp.where(d < max_exact, d, large)


def _norm_matmul_kernel(x_ref, g_ref, w_ref, o_ref, hn_ref):
    @pl.when(pl.program_id(1) == 0)
    def _():
        hn_ref[...] = _rms(x_ref[...], g_ref[...]).astype(BF16)

    o_ref[...] = jnp.dot(hn_ref[...], w_ref[...], preferred_element_type=F32).astype(o_ref.dtype)


def _norm_matmul(x, g, w, out_dtype, tm, tn):
    m, k = x.shape
    n = w.shape[1]
    return pl.pallas_call(
        _norm_matmul_kernel,
        grid=(m // tm, n // tn),
        in_specs=[pl.BlockSpec((tm, k), lambda i, j: (i, 0)),
                  pl.BlockSpec((1, k), lambda i, j: (0, 0)),
                  pl.BlockSpec((k, tn), lambda i, j: (0, j))],
        out_specs=pl.BlockSpec((tm, tn), lambda i, j: (i, j)),
        out_shape=jax.ShapeDtypeStruct((m, n), out_dtype),
        scratch_shapes=[pltpu.VMEM((tm, k), BF16)],
        compiler_params=_params(("parallel", "arbitrary")),
        name="norm_matmul",
    )(x, g.reshape(1, k), w)


def _ffn_kernel(x_ref, g_ref, wg_ref, wu_ref, wd_ref, o_ref, hn_ref, acc_ref):
    j = pl.program_id(1)

    @pl.when(j == 0)
    def _():
        hn_ref[...] = _rms(x_ref[...], g_ref[...]).astype(BF16)
        acc_ref[...] = jnp.zeros_like(acc_ref)

    h = hn_ref[...]
    gate = jnp.dot(h, wg_ref[...], preferred_element_type=F32)
    up = jnp.dot(h, wu_ref[...], preferred_element_type=F32)
    act = (gate * _sigmoid(gate) * up).astype(BF16)
    acc_ref[...] += jnp.dot(act, wd_ref[...], preferred_element_type=F32)

    @pl.when(j == pl.num_programs(1) - 1)
    def _():
        o_ref[...] = x_ref[...] + 0.5 * acc_ref[...]


def _ffn(x, g, w_gate, w_up, w_down, tm=512):
    m, d = x.shape
    f = w_gate.shape[1]
    tf = f // 2 if (f // 2) % LANES == 0 else f
    return pl.pallas_call(
        _ffn_kernel,
        grid=(m // tm, f // tf),
        in_specs=[pl.BlockSpec((tm, d), lambda i, j: (i, 0)),
                  pl.BlockSpec((1, d), lambda i, j: (0, 0)),
                  pl.BlockSpec((d, tf), lambda i, j: (0, j)),
                  pl.BlockSpec((d, tf), lambda i, j: (0, j)),
                  pl.BlockSpec((tf, d), lambda i, j: (j, 0))],
        out_specs=pl.BlockSpec((tm, d), lambda i, j: (i, 0)),
        out_shape=jax.ShapeDtypeStruct((m, d), F32),
        scratch_shapes=[pltpu.VMEM((tm, d), BF16), pltpu.VMEM((tm, d), F32)],
        compiler_params=_params(("parallel", "arbitrary")),
        name="ffn",
    )(x, g.reshape(1, d), w_gate.astype(BF16), w_up.astype(BF16), w_down.astype(BF16))


def _ple_kernel(x_ref, p_ref, g_ref, wg_ref, wp_ref, fg_ref, o_ref, *, final_norm):
    x = x_ref[...]
    h = _rms(x, g_ref[...]).astype(BF16)
    gate = _sigmoid(jnp.dot(h, wg_ref[...], preferred_element_type=F32))
    proj = jnp.dot(p_ref[...].astype(BF16), wp_ref[...], preferred_element_type=F32)
    y = x + gate * proj
    if final_norm:
        y = _rms(y, fg_ref[...])
    o_ref[...] = y


def _ple(x, p, g, w_gate, w_proj, final_g, final_norm, tm=512):
    m, d = x.shape
    dp = p.shape[1]
    return pl.pallas_call(
        functools.partial(_ple_kernel, final_norm=final_norm),
        grid=(m // tm,),
        in_specs=[pl.BlockSpec((tm, d), lambda i: (i, 0)),
                  pl.BlockSpec((tm, dp), lambda i: (i, 0)),
                  pl.BlockSpec((1, d), lambda i: (0, 0)),
                  pl.BlockSpec((d, d), lambda i: (0, 0)),
                  pl.BlockSpec((dp, d), lambda i: (0, 0)),
                  pl.BlockSpec((1, d), lambda i: (0, 0))],
        out_specs=pl.BlockSpec((tm, d), lambda i: (i, 0)),
        out_shape=jax.ShapeDtypeStruct((m, d), F32),
        compiler_params=_params(("parallel",)),
        name="ple",
    )(x, p, g.reshape(1, d), w_gate.astype(BF16), w_proj.astype(BF16), final_g.reshape(1, d))


def _dsa_prep_kernel(a_ref, cqg_ref, ckvg_ref, wq_ref, ql_ref, qi_ref, kvt_ref, kva_ref, kit_ref,
                     wit_ref):
    a = a_ref[0]
    tm = a.shape[0]
    cqn = _rms(a[:, :A_DCQ], cqg_ref[...]).astype(BF16)
    q = jnp.dot(cqn, wq_ref[...], preferred_element_type=F32)
    nl = A_HEADS * A_DC
    ql = (q[:, :nl] * (A_DC ** -0.5 * LOG2E)).astype(BF16)
    for h in range(A_HEADS):
        ql_ref[0, h] = ql[:, h * A_DC:(h + 1) * A_DC]
    qi = q[:, nl:].astype(BF16)
    for h in range(IDX_HEADS):
        qi_ref[0, h] = qi[:, h * IDX_DIM:(h + 1) * IDX_DIM]
    kv = _rms(a[:, A_DCQ:A_DCQ + A_DC], ckvg_ref[...])
    kvt_ref[0] = kv.T.astype(BF16)
    lane = lax.broadcasted_iota(I32, (tm, A_DC), 1)
    kva_ref[0, :, :A_DC] = kv.astype(BF16)
    kva_ref[0, :, A_DC:] = jnp.where(lane == 0, 1.0, 0.0).astype(BF16)
    o = A_DCQ + A_DC
    rest_t = a[:, o:o + LANES].T
    kit_ref[0] = rest_t[:IDX_DIM].astype(BF16)
    wit_ref[0] = rest_t[IDX_DIM:IDX_DIM + IDX_HEADS] * (IDX_HEADS * IDX_DIM) ** -0.5


def _dsa_prep(a_in, cq_g, ckv_g, w_q, tm=512):
    bsz, t, na = a_in.shape
    nq = w_q.shape[1]
    fixed = lambda b, i: (0, 0)
    return pl.pallas_call(
        _dsa_prep_kernel,
        grid=(bsz, t // tm),
        in_specs=[pl.BlockSpec((1, tm, na), lambda b, i: (b, i, 0)),
                  pl.BlockSpec((1, A_DCQ), fixed),
                  pl.BlockSpec((1, A_DC), fixed),
                  pl.BlockSpec((A_DCQ, nq), fixed)],
        out_specs=[pl.BlockSpec((1, A_HEADS, tm, A_DC), lambda b, i: (b, 0, i, 0)),
                   pl.BlockSpec((1, IDX_HEADS, tm, IDX_DIM), lambda b, i: (b, 0, i, 0)),
                   pl.BlockSpec((1, A_DC, tm), lambda b, i: (b, 0, i)),
                   pl.BlockSpec((1, tm, 2 * A_DC), lambda b, i: (b, i, 0)),
                   pl.BlockSpec((1, IDX_DIM, tm), lambda b, i: (b, 0, i)),
                   pl.BlockSpec((1, IDX_HEADS, tm), lambda b, i: (b, 0, i))],
        out_shape=[jax.ShapeDtypeStruct((bsz, A_HEADS, t, A_DC), BF16),
                   jax.ShapeDtypeStruct((bsz, IDX_HEADS, t, IDX_DIM), BF16),
                   jax.ShapeDtypeStruct((bsz, A_DC, t), BF16),
                   jax.ShapeDtypeStruct((bsz, t, 2 * A_DC), BF16),
                   jax.ShapeDtypeStruct((bsz, IDX_DIM, t), BF16),
                   jax.ShapeDtypeStruct((bsz, IDX_HEADS, t), F32)],
        compiler_params=_params(("parallel", "parallel")),
        name="dsa_prep",
    )(a_in, cq_g.reshape(1, -1), ckv_g.reshape(1, -1), w_q)


def _a_strip_off():
    return REL_MAX_DIST + A_KEY_TILE


def _a_strip_len():
    return _a_strip_off() + A_KEY_TILE


def _a_strip_kernel(tbl_ref, o_ref):
    h = pl.program_id(0)
    shape = (Q_BLOCK, _a_strip_len())
    d = (lax.broadcasted_iota(I32, shape, 0) + _a_strip_off()
         - lax.broadcasted_iota(I32, shape, 1))
    bucket = _rel_bucket(d)
    far = tbl_ref[REL_BUCKETS - 1, h]
    val = jnp.zeros(shape, F32)
    for b in range(REL_BUCKETS - 1):
        val = jnp.where(bucket == b, (tbl_ref[b, h] - far) * LOG2E, val)
    o_ref[0] = val


def _a_strips(tbl):
    return pl.pallas_call(
        _a_strip_kernel,
        grid=(A_HEADS,),
        in_specs=[pl.BlockSpec(memory_space=pltpu.SMEM)],
        out_specs=pl.BlockSpec((1, Q_BLOCK, _a_strip_len()), lambda h: (h, 0, 0)),
        out_shape=jax.ShapeDtypeStruct((A_HEADS, Q_BLOCK, _a_strip_len()), F32),
        compiler_params=_params(("arbitrary",)),
        name="dsa_bias_strips",
    )(tbl)


def _c_bias_kernel(tbl_ref, o_ref, *, dilation, col0):
    v = pl.program_id(0)
    h = pl.program_id(1)
    shape = (Q_BLOCK, 2 * Q_BLOCK)
    i = lax.broadcasted_iota(I32, shape, 0)
    j = lax.broadcasted_iota(I32, shape, 1)
    dist = Q_BLOCK + i - j
    bucket = _rel_bucket(dist * dilation)
    val = jnp.zeros(shape, F32)
    for b in range(REL_BUCKETS):
        val = jnp.where(bucket == b, tbl_ref[b, col0 + h], val)
    valid = (dist >= 0) & (dist <= Q_BLOCK) & ((j >= Q_BLOCK) | (v > 0))
    o_ref[0, 0] = jnp.where(valid, val, NEG_BIG)


def _c_bias(tbl, dilation, col0):
    return pl.pallas_call(
        functools.partial(_c_bias_kernel, dilation=dilation, col0=col0),
        grid=(2, C_HEADS),
        in_specs=[pl.BlockSpec(memory_space=pltpu.SMEM)],
        out_specs=pl.BlockSpec((1, 1, Q_BLOCK, 2 * Q_BLOCK), lambda v, h: (v, h, 0, 0)),
        out_shape=jax.ShapeDtypeStruct((2, C_HEADS, Q_BLOCK, 2 * Q_BLOCK), F32),
        compiler_params=_params(("arbitrary", "arbitrary")),
        name="dilated_bias",
    )(tbl)


def _dsa_kernel(qi_ref, wi_ref, kit_ref, q_ref, kvt_ref, kva_ref, strip_ref, o_ref,
                sc_ref, thr_ref, kaug_ref, qa_ref, m_ref, acc_ref, lg_ref, p_ref, al_ref, *, topk, seq):
    qb = pl.program_id(1)
    hg = pl.program_id(2)
    w_tile = A_KEY_TILE
    q0 = qb * Q_BLOCK
    n_tiles = q0 // w_tile + 1
    n_pairs = (n_tiles + 1) // 2
    cols = w_tile // LANES
    rows = A_HEAD_GROUP * Q_BLOCK

    @pl.when((qb == 0) & (hg == 0))
    def _load_keys():
        kaug_ref[:A_DC, :] = kvt_ref[0]
    lane_iota = lax.broadcasted_iota(I32, (Q_BLOCK, w_tile), 1)
    row_iota = lax.broadcasted_iota(I32, (Q_BLOCK, w_tile), 0)

    def tile_start(kt):
        return pl.multiple_of(kt * w_tile, w_tile)

    @pl.when(hg == 0)
    def _select():
        qi = qi_ref[0].reshape(IDX_HEADS * Q_BLOCK, IDX_DIM)
        wcols = [wi_ref[0, h] for h in range(IDX_HEADS)]

        def score_tile(kt, carry):
            k0 = tile_start(kt)
            r = jnp.dot(qi, kit_ref[0, :, pl.ds(k0, w_tile)], preferred_element_type=F32)
            r = jnp.maximum(r, 0.0).reshape(IDX_HEADS, Q_BLOCK, w_tile)
            s = r[0] * wcols[0]
            for h in range(1, IDX_HEADS):
                s = s + r[h] * wcols[h]
            s = jnp.where(s == 0.0, 0.0, s)
            s = jnp.where(k0 + lane_iota <= q0 + row_iota, s, -jnp.inf)
            bits = pltpu.bitcast(s, I32)
            sc_ref[:, pl.ds(k0, w_tile)] = bits ^ ((bits >> 31) & 0x7FFFFFFF)
            return carry

        lax.fori_loop(0, n_tiles, score_tile, 0)

        def count(pred):
            def body(kt, cnt):
                k0 = tile_start(kt)
                blk = sc_ref[:, pl.ds(k0, w_tile)]
                for c in range(cols):
                    hit = pred(blk[:, c * LANES:(c + 1) * LANES], k0 + c * LANES)
                    cnt = cnt + jnp.where(hit, 1, 0)
                return cnt
            cnt = lax.fori_loop(0, n_tiles, body, jnp.zeros((Q_BLOCK, LANES), I32))
            return jnp.sum(cnt, axis=1, keepdims=True)

        def bcast(v):
            return jnp.broadcast_to(v, (Q_BLOCK, LANES))

        def kth_bit(i, t):
            cand = t | lax.shift_left(jnp.int32(1), 31 - i)
            cand_b = bcast(cand ^ INT_MIN)
            c = count(lambda blk, _: blk >= cand_b)
            return jnp.where(c >= topk, cand, t)

        def bounds_tile(kt, carry):
            k0 = tile_start(kt)
            blk = sc_ref[:, pl.ds(k0, w_tile)]
            best = list(carry)
            for c in range(cols):
                best[c % 2] = jnp.maximum(best[c % 2], blk[:, c * LANES:(c + 1) * LANES])
            return tuple(best)

        floor = jnp.full((Q_BLOCK, LANES), INT_MIN, I32)
        best_even, best_odd = lax.fori_loop(0, n_tiles, bounds_tile, (floor, floor))

        def as_float(key):
            return pltpu.bitcast(key ^ ((key >> 31) & 0x7FFFFFFF), F32)

        def as_key(val):
            bits = pltpu.bitcast(val, I32)
            return bits ^ ((bits >> 31) & 0x7FFFFFFF)

        lo_key = as_key(jnp.broadcast_to(
            jnp.min(as_float(jnp.minimum(best_even, best_odd)), axis=1, keepdims=True), (Q_BLOCK, LANES)))
        hi_key = as_key(jnp.broadcast_to(
            jnp.max(as_float(jnp.maximum(best_even, best_odd)), axis=1, keepdims=True), (Q_BLOCK, LANES)))
        shared = lax.clz(lo_key ^ hi_key).astype(F32)
        n_known = jnp.min(shared).astype(I32) if topk <= 2 * LANES else jnp.int32(0)
        known = jnp.where(n_known == 0, 0, lax.shift_left(jnp.int32(-1), jnp.minimum(32 - n_known, 31)))
        t0 = ((hi_key ^ INT_MIN) & known)[:, 0:1]
        t = lax.fori_loop(n_known, 32, kth_bit, t0)
        thr = jnp.maximum(t ^ INT_MIN, KEY_NEG_INF + 1)
        thr_b = bcast(thr)
        thr_ref[:, 0:1] = thr
        n_gt = count(lambda blk, _: blk > thr_b)
        n_eq = count(lambda blk, _: blk == thr_b)
        need = topk - n_gt
        thr_ref[:, 1:2] = jnp.full((Q_BLOCK, 1), 2 * seq, I32)

        @pl.when(jnp.max(n_eq - need) > 0)
        def _ties():
            lane = lax.broadcasted_iota(I32, (Q_BLOCK, LANES), 1)
            nbits = (2 * seq - 1).bit_length()

            def lim_bit(i, lim):
                cand = lim | lax.shift_left(jnp.int32(1), nbits - 1 - i)
                cand_b = bcast(cand)
                c = count(lambda blk, k0: jnp.where(blk == thr_b, k0 + lane, 2 * seq) < cand_b)
                return jnp.where(c <= need, cand, lim)

            thr_ref[:, 1:2] = lax.fori_loop(0, nbits, lim_bit, jnp.zeros((Q_BLOCK, 1), I32))

        lim = thr_ref[:, 1:2]

        def mask_tile(kt, carry):
            k0 = tile_start(kt)
            blk = sc_ref[:, pl.ds(k0, w_tile)]
            bound = jnp.where(k0 + lane_iota < lim, thr, thr + 1)
            add = jnp.where(blk >= bound, 0.0, NEG_BIG).astype(F32)
            add = jnp.where(kt < n_tiles, add, NEG_BIG)
            kaug_ref[A_DC:, pl.ds(k0, w_tile)] = add.astype(BF16)
            return carry

        lax.fori_loop(0, 2 * n_pairs, mask_tile, 0)

    chunk = 32
    eye = (lax.broadcasted_iota(I32, (Q_BLOCK, Q_BLOCK), 0)
           == lax.broadcasted_iota(I32, (Q_BLOCK, Q_BLOCK), 1))
    onehot = jnp.where(eye, 1.0, 0.0).astype(BF16)
    for hh in range(A_HEAD_GROUP):
        qa_ref[hh * Q_BLOCK:(hh + 1) * Q_BLOCK, :A_DC] = q_ref[0, hh]
        qa_ref[hh * Q_BLOCK:(hh + 1) * Q_BLOCK, A_DC:] = onehot
    m_ref[...] = jnp.full(m_ref.shape, NEG_BIG, F32)
    acc_ref[...] = jnp.zeros_like(acc_ref)

    def stage_a(kt, slot):
        k0 = tile_start(kt)
        lg_ref[slot] = jnp.dot(qa_ref[...], kaug_ref[:, pl.ds(k0, w_tile)], preferred_element_type=F32)

    def stage_b(kt, slot, near):
        k0 = tile_start(kt)
        l0 = pl.multiple_of(jnp.clip(_a_strip_off() - (q0 - k0), 0, _a_strip_off()), LANES)
        for hh in range(A_HEAD_GROUP):
            for r0 in range(0, Q_BLOCK, chunk):
                rs = pl.ds(hh * Q_BLOCK + r0, chunk)
                x = lg_ref[slot, rs, :]
                if near:
                    x = x + strip_ref[hh, pl.ds(r0, chunk), pl.ds(l0, w_tile)]
                m_old = m_ref[rs, :]
                m_new = jnp.maximum(m_old, jnp.max(x, axis=1, keepdims=True))
                for c in range(cols):
                    cs = slice(c * LANES, (c + 1) * LANES)
                    p_ref[slot, rs, cs] = jnp.exp2(x[:, cs] - m_new).astype(BF16)
                al_ref[slot, rs, :] = jnp.exp2(m_old - m_new)
                m_ref[rs, :] = m_new

    def stage_c(kt, slot):
        k0 = tile_start(kt)
        pv = jnp.dot(p_ref[slot], kva_ref[0, pl.ds(k0, w_tile), :], preferred_element_type=F32)
        al = al_ref[slot]
        acc_ref[:, :A_DC] = acc_ref[:, :A_DC] * al + pv[:, :A_DC]
        acc_ref[:, A_DC:] = acc_ref[:, A_DC:] * al + pv[:, A_DC:]

    stage_a(0, 0)
    stage_a(1, 1)
    stage_b(0, 0, True)

    def pair_body(i, near):
        kt = 2 * i
        stage_a(kt, 0)
        stage_b(kt - 1, 1, near)
        stage_c(kt - 2, 0)
        stage_a(kt + 1, 1)
        stage_b(kt, 0, near)
        stage_c(kt - 1, 1)

    def run_pairs(lo, hi, near):
        def two_pairs(j, carry):
            pair_body(lo + 2 * j, near)
            pair_body(lo + 2 * j + 1, near)
            return carry

        lax.fori_loop(0, (hi - lo) // 2, two_pairs, 0)

        @pl.when((hi - lo) % 2 == 1)
        def _():
            pair_body(hi - 1, near)

    n_far = jnp.maximum(q0 - (REL_MAX_DIST - 1), 0) // w_tile
    first_near = jnp.maximum((n_far + 1) // 2, 1)
    run_pairs(1, first_near, False)
    run_pairs(first_near, n_pairs, True)
    last = 2 * n_pairs - 1
    stage_b(last, 1, True)
    stage_c(last - 1, 0)
    stage_c(last, 1)

    acc = acc_ref[...]
    out = acc[:, :A_DC] * (1.0 / acc[:, A_DC:A_DC + 1])
    for hh in range(A_HEAD_GROUP):
        o_ref[0, :, hh * A_DC:(hh + 1) * A_DC] = out[hh * Q_BLOCK:(hh + 1) * Q_BLOCK].astype(o_ref.dtype)


def _dsa_attention(qi, wi, kit, q, kvt, kva, strips, topk):
    bsz, _, t, _ = q.shape
    assert t % (2 * A_KEY_TILE) == 0
    groups = A_HEADS // A_HEAD_GROUP
    rows = A_HEAD_GROUP * Q_BLOCK
    resident = dict(pipeline_mode=pl.Buffered(1))
    return pl.pallas_call(
        functools.partial(_dsa_kernel, topk=topk, seq=t),
        grid=(bsz, t // Q_BLOCK, groups),
        in_specs=[
            pl.BlockSpec((1, IDX_HEADS, Q_BLOCK, IDX_DIM), lambda b, i, g: (b, 0, i, 0)),
            pl.BlockSpec((1, IDX_HEADS, Q_BLOCK, 1), lambda b, i, g: (b, 0, i, 0)),
            pl.BlockSpec((1, IDX_DIM, t), lambda b, i, g: (b, 0, 0), **resident),
            pl.BlockSpec((1, A_HEAD_GROUP, Q_BLOCK, A_DC), lambda b, i, g: (b, g, i, 0)),
            pl.BlockSpec((1, A_DC, t), lambda b, i, g: (b, 0, 0), **resident),
            pl.BlockSpec((1, t, 2 * A_DC), lambda b, i, g: (b, 0, 0), **resident),
            pl.BlockSpec((A_HEAD_GROUP, Q_BLOCK, _a_strip_len()), lambda b, i, g: (g, 0, 0)),
        ],
        out_specs=pl.BlockSpec((1, Q_BLOCK, A_HEAD_GROUP * A_DC), lambda b, i, g: (b, i, g)),
        out_shape=jax.ShapeDtypeStruct((bsz, t, A_HEADS * A_DC), BF16),
        scratch_shapes=[pltpu.VMEM((Q_BLOCK, t), I32),
                        pltpu.VMEM((Q_BLOCK, LANES), I32),
                        pltpu.VMEM((2 * A_DC, t), BF16),
                        pltpu.VMEM((rows, 2 * A_DC), BF16),
                        pltpu.VMEM((rows, LANES), F32),
                        pltpu.VMEM((rows, 2 * A_DC), F32),
                        pltpu.VMEM((2, rows, A_KEY_TILE), F32),
                        pltpu.VMEM((2, rows, A_KEY_TILE), BF16),
                        pltpu.VMEM((2, rows, LANES), F32)],
        compiler_params=_params(("parallel", "arbitrary", "arbitrary")),
        name="dsa_attention",
    )(qi, wi, kit, q, kvt, kva, strips)


def _dsa_mixer(a_in, cq_g, ckv_g, w_uq, w_iq, strips, bsz, t):
    w_q = jnp.concatenate([w_uq, w_iq], axis=1).astype(BF16)
    ql, qi, kvt, kva, kit, wit = _dsa_prep(a_in.reshape(bsz, t, -1), cq_g, ckv_g, w_q)
    topk = min(TOPK_MAX, t // 4)
    return _dsa_attention(qi, wit[..., None], kit, ql, kvt, kva, strips, topk)


def _split3(x):
    hi = x.astype(BF16)
    r1 = x - hi.astype(F32)
    mid = r1.astype(BF16)
    lo = (r1 - mid.astype(F32)).astype(BF16)
    return hi, mid, lo


def _ssd_kernel(z_ref, xbc_ref, dt_ref, cw_ref, cb_ref, dtb_ref, alog_ref, dsk_ref, ng_ref, y_ref,
                ext_ref, st_ref, yb_ref, *, d_inner):
    c = pl.program_id(1)
    q = SSM_CHUNK
    n_state = SSM_STATE
    p_dim = SSM_HEADDIM
    hpg = d_inner // p_dim // SSM_GROUPS
    hist = 8

    @pl.when(c == 0)
    def _():
        ext_ref[0:hist, :] = jnp.zeros((hist, ext_ref.shape[1]), F32)
        st_ref[...] = jnp.zeros_like(st_ref)

    ext_ref[hist:hist + q, :] = xbc_ref[0]
    conv = cb_ref[...]
    for k in range(SSM_CONV):
        lo = hist - (SSM_CONV - 1) + k
        conv = conv + cw_ref[k:k + 1, :] * ext_ref[lo:lo + q, :]
    ext_ref[0:hist, :] = ext_ref[q:q + hist, :]
    u = conv * _sigmoid(conv)
    gn = SSM_GROUPS * n_state
    xs = u[:, :d_inner]
    bm = u[:, d_inner:d_inner + gn]
    cm = u[:, d_inner + gn:]

    raw = dt_ref[0] + dtb_ref[...]
    dt = jnp.maximum(raw, 0.0) + jnp.log(1.0 + jnp.exp(-jnp.abs(raw)))
    dta = dt * (-jnp.exp(alog_ref[...]))
    ii = lax.broadcasted_iota(I32, (q, q), 0)
    jj = lax.broadcasted_iota(I32, (q, q), 1)
    tril = ii >= jj
    tri = jnp.where(tril, 1.0, 0.0).astype(BF16)
    cum = sum(jnp.dot(tri, part, preferred_element_type=F32) for part in _split3(dta))
    cum_t = cum.T
    dt_t = dt.T
    last_t = cum_t[:, q - 1:q]
    ecum = jnp.exp(cum)
    wst_t = jnp.exp(last_t - cum_t) * dt_t
    elast = jnp.exp(last_t)

    for g in range(SSM_GROUPS):
        bc = bm[:, g * n_state:(g + 1) * n_state]
        cc = cm[:, g * n_state:(g + 1) * n_state]
        bc16 = bc.astype(BF16)
        cc16 = cc.astype(BF16)
        cb = lax.dot_general(cc16, bc16, (((1,), (1,)), ((), ())), preferred_element_type=F32)
        xs_t = xs[:, g * hpg * p_dim:(g + 1) * hpg * p_dim].T
        for hh in range(hpg):
            h = g * hpg + hh
            seg = cum[:, h:h + 1] - cum_t[h:h + 1, :]
            decay = jnp.exp(jnp.where(tril, seg, -jnp.inf))
            wgt = (decay * cb * dt_t[h:h + 1, :]).astype(BF16)
            xc16 = xs[:, h * p_dim:(h + 1) * p_dim].astype(BF16)
            st = st_ref[h]
            y_off = lax.dot_general(cc16, st.astype(BF16), (((1,), (1,)), ((), ())),
                                    preferred_element_type=F32)
            y = jnp.dot(wgt, xc16, preferred_element_type=F32) + y_off * ecum[:, h:h + 1]
            wx_t = (xs_t[hh * p_dim:(hh + 1) * p_dim, :] * wst_t[h:h + 1, :]).astype(BF16)
            st_ref[h] = st * elast[h:h + 1, :] + jnp.dot(wx_t, bc16, preferred_element_type=F32)
            yb_ref[:, h * p_dim:(h + 1) * p_dim] = y

    z = z_ref[0]
    y = (yb_ref[...] + xs * dsk_ref[...]) * (z * _sigmoid(z))
    gw = hpg * p_dim
    for g in range(SSM_GROUPS):
        yg = y[:, g * gw:(g + 1) * gw]
        yn = yg * lax.rsqrt(jnp.mean(yg * yg, axis=-1, keepdims=True) + EPS)
        y_ref[0, :, g * gw:(g + 1) * gw] = (yn * ng_ref[:, g * gw:(g + 1) * gw]).astype(y_ref.dtype)


def _ssd_mixer(z, xbc, dt, conv_w, conv_b, dt_bias, a_log, d_skip, norm_g):
    bsz, t, d_inner = z.shape
    conv_dim = xbc.shape[-1]
    heads = d_inner // SSM_HEADDIM
    pad = LANES - heads
    dtb = jnp.pad(dt_bias, (0, pad)).reshape(1, LANES)
    alog = jnp.pad(a_log, (0, pad)).reshape(1, LANES)
    dsk = jnp.repeat(d_skip, SSM_HEADDIM).reshape(1, d_inner)
    fixed = lambda b, c: (0, 0)
    return pl.pallas_call(
        functools.partial(_ssd_kernel, d_inner=d_inner),
        grid=(bsz, t // SSM_CHUNK),
        in_specs=[pl.BlockSpec((1, SSM_CHUNK, d_inner), lambda b, c: (b, c, 0)),
                  pl.BlockSpec((1, SSM_CHUNK, conv_dim), lambda b, c: (b, c, 0)),
                  pl.BlockSpec((1, SSM_CHUNK, LANES), lambda b, c: (b, c, 0)),
                  pl.BlockSpec((SSM_CONV, conv_dim), fixed),
                  pl.BlockSpec((1, conv_dim), fixed),
                  pl.BlockSpec((1, LANES), fixed),
                  pl.BlockSpec((1, LANES), fixed),
                  pl.BlockSpec((1, d_inner), fixed),
                  pl.BlockSpec((1, d_inner), fixed)],
        out_specs=pl.BlockSpec((1, SSM_CHUNK, d_inner), lambda b, c: (b, c, 0)),
        out_shape=jax.ShapeDtypeStruct((bsz, t, d_inner), BF16),
        scratch_shapes=[pltpu.VMEM((SSM_CHUNK + 8, conv_dim), F32),
                        pltpu.VMEM((heads, SSM_HEADDIM, SSM_STATE), F32),
                        pltpu.VMEM((SSM_CHUNK, d_inner), F32)],
        compiler_params=_params(("parallel", "arbitrary")),
        name="ssd",
    )(z, xbc, dt, conv_w.reshape(SSM_CONV, conv_dim), conv_b.reshape(1, conv_dim), dtb, alog, dsk,
      norm_g.reshape(1, d_inner))


def _dilated_kernel(q_ref, kc_ref, kp_ref, vc_ref, vp_ref, b_ref, o_ref, l_ref):
    d = C_HEADDIM
    q = q_ref[0]
    k = jnp.concatenate([kp_ref[0], kc_ref[0]], axis=0)
    v = jnp.concatenate([vp_ref[0], vc_ref[0]], axis=0)
    for h in range(C_HEADS):
        sl = slice(h * d, (h + 1) * d)
        lg = lax.dot_general(q[:, sl], k[:, sl], (((1,), (1,)), ((), ())), preferred_element_type=F32)
        x = lg * d ** -0.5 + b_ref[0, h]
        m = jnp.max(x, axis=1, keepdims=True)
        p = jnp.exp(x - m)
        s = jnp.sum(p, axis=1, keepdims=True)
        o = jnp.dot(p.astype(BF16), v[:, sl], preferred_element_type=F32)
        o_ref[0, :, sl] = o * (1.0 / s)
        l_ref[0, :, sl] = jnp.broadcast_to(m + jnp.log(s), (Q_BLOCK, d))


def _dilated_group(qkv, bias, g, dilation):
    bsz, t, cols = qkv.shape
    hd = C_HEADS * C_HEADDIM
    per_tok = cols // hd
    tr = t // dilation
    view = qkv.reshape(bsz, tr, dilation * cols)
    col = lambda c, j: c * per_tok + g * 3 + j
    cur = lambda j: (lambda b, c, n: (b, n, col(c, j)))
    prev = lambda j: (lambda b, c, n: (b, jnp.maximum(n - 1, 0), col(c, j)))
    blk = (1, Q_BLOCK, hd)
    out_spec = pl.BlockSpec(blk, lambda b, c, n: (b, n, c))
    o, lse = pl.pallas_call(
        _dilated_kernel,
        grid=(bsz, dilation, tr // Q_BLOCK),
        in_specs=[pl.BlockSpec(blk, cur(0)), pl.BlockSpec(blk, cur(1)), pl.BlockSpec(blk, prev(1)),
                  pl.BlockSpec(blk, cur(2)), pl.BlockSpec(blk, prev(2)),
                  pl.BlockSpec((1, C_HEADS, Q_BLOCK, 2 * Q_BLOCK),
                               lambda b, c, n: (jnp.minimum(n, 1), 0, 0, 0))],
        out_specs=[out_spec, out_spec],
        out_shape=[jax.ShapeDtypeStruct((bsz, tr, dilation * hd), F32)] * 2,
        compiler_params=_params(("parallel", "parallel", "arbitrary")),
        name="dilated_attention",
    )(view, view, view, view, view, bias)
    return o.reshape(bsz * t, hd), lse.reshape(bsz * t, hd)


def _merge_kernel(x_ref, ya_ref, yb_ref, o1_ref, o2_ref, o3_ref, l1_ref, l2_ref, l3_ref,
                  ga_ref, gb_ref, gc_ref, wa_ref, wb_ref, wc_ref, wo_ref, out_ref):
    l1, l2, l3 = l1_ref[...], l2_ref[...], l3_ref[...]
    lm = jnp.maximum(jnp.maximum(l1, l2), l3)
    e1, e2, e3 = jnp.exp(l1 - lm), jnp.exp(l2 - lm), jnp.exp(l3 - lm)
    yc = (e1 * o1_ref[...] + e2 * o2_ref[...] + e3 * o3_ref[...]) * (1.0 / (e1 + e2 + e3))
    pa = jnp.dot(ya_ref[...], wa_ref[...], preferred_element_type=F32)
    pb = jnp.dot(yb_ref[...], wb_ref[...], preferred_element_type=F32)
    pc = jnp.dot(yc.astype(BF16), wc_ref[...], preferred_element_type=F32)
    merged = _sigmoid(ga_ref[...]) * pa + _sigmoid(gb_ref[...]) * pb + _sigmoid(gc_ref[...]) * pc
    out_ref[...] = x_ref[...] + jnp.dot(merged.astype(BF16), wo_ref[...], preferred_element_type=F32)


def _merge(x, ya, yb, oc, lc, gates, w_a, w_b, w_c, w_o, tm=256):
    m, d = x.shape
    row = lambda i: (i, 0)
    fixed = lambda i: (0, 0)
    full = lambda a: pl.BlockSpec(a.shape, fixed)
    rows = lambda a: pl.BlockSpec((tm, a.shape[1]), row)
    weights = [w_a.astype(BF16), w_b.astype(BF16), w_c.astype(BF16), w_o.astype(BF16)]
    return pl.pallas_call(
        _merge_kernel,
        grid=(m // tm,),
        in_specs=[rows(x), rows(ya), rows(yb)] + [rows(a) for a in oc] + [rows(a) for a in lc]
                 + [pl.BlockSpec((tm, d), lambda i, j=j: (i, j)) for j in range(3)]
                 + [full(w) for w in weights],
        out_specs=pl.BlockSpec((tm, d), row),
        out_shape=jax.ShapeDtypeStruct((m, d), F32),
        compiler_params=_params(("parallel",)),
        name="merge",
    )(x, ya, yb, *oc, *lc, gates, gates, gates, *weights)


def _pad_cols(w, n):
    return jnp.pad(w, ((0, 0), (0, n - w.shape[1])))


def kernel(x, p, rel_bias, final_norm, ffn1_norm, ffn1_gate, ffn1_up, ffn1_down, mix_norm, w_in,
           cq_norm, ckv_norm, w_uq, w_iq, conv_w, conv_b, dt_bias, a_log, d_skip, ssm_norm,
           w_out_a, w_out_b, w_out_c, w_o, ffn2_norm, ffn2_gate, ffn2_up, ffn2_down, ple_norm,
           w_ple_gate, w_ple_proj):
    bsz, t, d_model = x.shape
    depth = w_in.shape[0]
    m = bsz * t
    d_inner = ssm_norm.shape[1]
    conv_dim = conv_b.shape[1]
    ssm_heads = dt_bias.shape[1]
    c_cols = len(C_GROUPS) * 3 * C_HEADS * C_HEADDIM
    sizes = (A_DCQ, A_DC, IDX_DIM, IDX_HEADS, d_inner, conv_dim, ssm_heads, c_cols,
             d_model, d_model, d_model)
    offs = [0]
    for s in sizes:
        offs.append(offs[-1] + s)

    strips = _a_strips(rel_bias)
    c_bias = [_c_bias(rel_bias, dil, A_HEADS + g * C_HEADS) for g, (_, dil) in enumerate(C_GROUPS)]

    x = x.reshape(m, d_model)
    for i in range(depth):
        x = _ffn(x, ffn1_norm[i], ffn1_gate[i], ffn1_up[i], ffn1_down[i])

        w = w_in[i].astype(BF16)
        g_mix = mix_norm[i]
        a_in = _norm_matmul(x, g_mix, _pad_cols(w[:, offs[0]:offs[4]], 4 * LANES), F32, 512, 4 * LANES)
        z = _norm_matmul(x, g_mix, w[:, offs[4]:offs[5]], F32, 1024, 1024)
        xbc = _norm_matmul(x, g_mix, w[:, offs[5]:offs[6]], F32, 1024, 1024)
        dt = _norm_matmul(x, g_mix, _pad_cols(w[:, offs[6]:offs[7]], LANES), F32, 512, LANES)
        qkv = _norm_matmul(x, g_mix, w[:, offs[7]:offs[8]], BF16, 1024, 1536)
        gates = _norm_matmul(x, g_mix, w[:, offs[8]:offs[11]], F32, 1024, 1024)

        ya = _dsa_mixer(a_in, cq_norm[i], ckv_norm[i], w_uq[i], w_iq[i], strips, bsz, t)
        yb = _ssd_mixer(z.reshape(bsz, t, d_inner), xbc.reshape(bsz, t, conv_dim),
                        dt.reshape(bsz, t, LANES), conv_w[i], conv_b[i], dt_bias[i], a_log[i],
                        d_skip[i], ssm_norm[i])
        qkv = qkv.reshape(bsz, t, c_cols)
        oc, lc = zip(*[_dilated_group(qkv, c_bias[g], g, dil) for g, (_, dil) in enumerate(C_GROUPS)])
        x = _merge(x, ya.reshape(m, -1), yb.reshape(m, -1), oc, lc, gates,
                   w_out_a[i], w_out_b[i], w_out_c[i], w_o[i])

        x = _ffn(x, ffn2_norm[i], ffn2_gate[i], ffn2_up[i], ffn2_down[i])
        x = _ple(x, p[i].reshape(m, -1), ple_norm[i], w_ple_gate[i], w_ple_proj[i], final_norm,
                 final_norm=(i == depth - 1))
    return x.reshape(bsz, t, d_model)
```

```python
import functools
import math

import jax
import jax.numpy as jnp
from jax import lax
from jax.experimental import pallas as pl
from jax.experimental.pallas import tpu as pltpu

F32 = jnp.float32
BF16 = jnp.bfloat16
I32 = jnp.int32

EPS = 1e-6
LANES = 128
Q_BLOCK = 128
VMEM_LIMIT = 56 * 1024 * 1024

A_HEADS = 16
A_DC = 128
A_DCQ = 256
IDX_HEADS = 8
IDX_DIM = 64
TOPK_MAX = 256
A_HEAD_GROUP = 4
A_KEY_TILE = 512
SSM_HEADDIM = 64
SSM_GROUPS = 8
SSM_STATE = 128
SSM_CONV = 4
SSM_CHUNK = 128
C_GROUPS = ((128, 1), (512, 4), (2048, 16))
C_HEADS = 8
C_HEADDIM = 64
REL_BUCKETS = 32
REL_MAX_DIST = 2048

LOG2E = 1.4426950408889634
NEG_BIG = -1e30
KEY_NEG_INF = -2139095041
INT_MIN = -2147483648


def _params(sem, vmem=VMEM_LIMIT):
    return pltpu.CompilerParams(dimension_semantics=sem, vmem_limit_bytes=vmem)


def _rms(x, g):
    return x * lax.rsqrt(jnp.mean(x * x, axis=-1, keepdims=True) + EPS) * g


def _sigmoid(x):
    return 1.0 / (1.0 + jnp.exp(-x))


def _rel_bucket(d):
    max_exact = REL_BUCKETS // 2
    d = jnp.maximum(d, 0)
    df = jnp.maximum(d, 1).astype(F32)
    large = max_exact + (jnp.log(df / max_exact) / math.log(REL_MAX_DIST / max_exact)
                         * (REL_BUCKETS - max_exact)).astype(I32)
    large = jnp.minimum(large, REL_BUCKETS - 1)
    return jnp.where(d < max_exact, d, large)


def _norm_matmul_kernel(x_ref, g_ref, w_ref, o_ref, hn_ref):
    @pl.when(pl.program_id(1) == 0)
    def _():
        hn_ref[...] = _rms(x_ref[...], g_ref[...]).astype(BF16)

    o_ref[...] = jnp.dot(hn_ref[...], w_ref[...], preferred_element_type=F32).astype(o_ref.dtype)


def _norm_matmul(x, g, w, out_dtype, tm, tn):
    m, k = x.shape
    n = w.shape[1]
    return pl.pallas_call(
        _norm_matmul_kernel,
        grid=(m // tm, n // tn),
        in_specs=[pl.BlockSpec((tm, k), lambda i, j: (i, 0)),
                  pl.BlockSpec((1, k), lambda i, j: (0, 0)),
                  pl.BlockSpec((k, tn), lambda i, j: (0, j))],
        out_specs=pl.BlockSpec((tm, tn), lambda i, j: (i, j)),
        out_shape=jax.ShapeDtypeStruct((m, n), out_dtype),
        scratch_shapes=[pltpu.VMEM((tm, k), BF16)],
        compiler_params=_params(("parallel", "arbitrary")),
        name="norm_matmul",
    )(x, g.reshape(1, k), w)


def _ffn_kernel(x_ref, g_ref, wg_ref, wu_ref, wd_ref, o_ref, hn_ref, acc_ref):
    j = pl.program_id(1)

    @pl.when(j == 0)
    def _():
        hn_ref[...] = _rms(x_ref[...], g_ref[...]).astype(BF16)
        acc_ref[...] = jnp.zeros_like(acc_ref)

    h = hn_ref[...]
    gate = jnp.dot(h, wg_ref[...], preferred_element_type=F32)
    up = jnp.dot(h, wu_ref[...], preferred_element_type=F32)
    act = (gate * _sigmoid(gate) * up).astype(BF16)
    acc_ref[...] += jnp.dot(act, wd_ref[...], preferred_element_type=F32)

    @pl.when(j == pl.num_programs(1) - 1)
    def _():
        o_ref[...] = x_ref[...] + 0.5 * acc_ref[...]


def _ffn(x, g, w_gate, w_up, w_down, tm=512):
    m, d = x.shape
    f = w_gate.shape[1]
    tf = f // 2 if (f // 2) % LANES == 0 else f
    return pl.pallas_call(
        _ffn_kernel,
        grid=(m // tm, f // tf),
        in_specs=[pl.BlockSpec((tm, d), lambda i, j: (i, 0)),
                  pl.BlockSpec((1, d), lambda i, j: (0, 0)),
                  pl.BlockSpec((d, tf), lambda i, j: (0, j)),
                  pl.BlockSpec((d, tf), lambda i, j: (0, j)),
                  pl.BlockSpec((tf, d), lambda i, j: (j, 0))],
        out_specs=pl.BlockSpec((tm, d), lambda i, j: (i, 0)),
        out_shape=jax.ShapeDtypeStruct((m, d), F32),
        scratch_shapes=[pltpu.VMEM((tm, d), BF16), pltpu.VMEM((tm, d), F32)],
        compiler_params=_params(("parallel", "arbitrary")),
        name="ffn",
    )(x, g.reshape(1, d), w_gate.astype(BF16), w_up.astype(BF16), w_down.astype(BF16))


def _ple_kernel(x_ref, p_ref, g_ref, wg_ref, wp_ref, fg_ref, o_ref, *, final_norm):
    x = x_ref[...]
    h = _rms(x, g_ref[...]).astype(BF16)
    gate = _sigmoid(jnp.dot(h, wg_ref[...], preferred_element_type=F32))
    proj = jnp.dot(p_ref[...].astype(BF16), wp_ref[...], preferred_element_type=F32)
    y = x + gate * proj
    if final_norm:
        y = _rms(y, fg_ref[...])
    o_ref[...] = y


def _ple(x, p, g, w_gate, w_proj, final_g, final_norm, tm=512):
    m, d = x.shape
    dp = p.shape[1]
    return pl.pallas_call(
        functools.partial(_ple_kernel, final_norm=final_norm),
        grid=(m // tm,),
        in_specs=[pl.BlockSpec((tm, d), lambda i: (i, 0)),
                  pl.BlockSpec((tm, dp), lambda i: (i, 0)),
                  pl.BlockSpec((1, d), lambda i: (0, 0)),
                  pl.BlockSpec((d, d), lambda i: (0, 0)),
                  pl.BlockSpec((dp, d), lambda i: (0, 0)),
                  pl.BlockSpec((1, d), lambda i: (0, 0))],
        out_specs=pl.BlockSpec((tm, d), lambda i: (i, 0)),
        out_shape=jax.ShapeDtypeStruct((m, d), F32),
        compiler_params=_params(("parallel",)),
        name="ple",
    )(x, p, g.reshape(1, d), w_gate.astype(BF16), w_proj.astype(BF16), final_g.reshape(1, d))


def _dsa_prep_kernel(a_ref, cqg_ref, ckvg_ref, wq_ref, ql_ref, qi_ref, kvt_ref, kva_ref, kit_ref,
                     wit_ref):
    a = a_ref[0]
    tm = a.shape[0]
    cqn = _rms(a[:, :A_DCQ], cqg_ref[...]).astype(BF16)
    q = jnp.dot(cqn, wq_ref[...], preferred_element_type=F32)
    nl = A_HEADS * A_DC
    ql = (q[:, :nl] * (A_DC ** -0.5 * LOG2E)).astype(BF16)
    for h in range(A_HEADS):
        ql_ref[0, h] = ql[:, h * A_DC:(h + 1) * A_DC]
    qi = q[:, nl:].astype(BF16)
    for h in range(IDX_HEADS):
        qi_ref[0, h] = qi[:, h * IDX_DIM:(h + 1) * IDX_DIM]
    kv = _rms(a[:, A_DCQ:A_DCQ + A_DC], ckvg_ref[...])
    kvt_ref[0] = kv.T.astype(BF16)
    lane = lax.broadcasted_iota(I32, (tm, A_DC), 1)
    kva_ref[0, :, :A_DC] = kv.astype(BF16)
    kva_ref[0, :, A_DC:] = jnp.where(lane == 0, 1.0, 0.0).astype(BF16)
    o = A_DCQ + A_DC
    rest_t = a[:, o:o + LANES].T
    kit_ref[0] = rest_t[:IDX_DIM].astype(BF16)
    wit_ref[0] = rest_t[IDX_DIM:IDX_DIM + IDX_HEADS] * (IDX_HEADS * IDX_DIM) ** -0.5


def _dsa_prep(a_in, cq_g, ckv_g, w_q, tm=512):
    bsz, t, na = a_in.shape
    nq = w_q.shape[1]
    fixed = lambda b, i: (0, 0)
    return pl.pallas_call(
        _dsa_prep_kernel,
        grid=(bsz, t // tm),
        in_specs=[pl.BlockSpec((1, tm, na), lambda b, i: (b, i, 0)),
                  pl.BlockSpec((1, A_DCQ), fixed),
                  pl.BlockSpec((1, A_DC), fixed),
                  pl.BlockSpec((A_DCQ, nq), fixed)],
        out_specs=[pl.BlockSpec((1, A_HEADS, tm, A_DC), lambda b, i: (b, 0, i, 0)),
                   pl.BlockSpec((1, IDX_HEADS, tm, IDX_DIM), lambda b, i: (b, 0, i, 0)),
                   pl.BlockSpec((1, A_DC, tm), lambda b, i: (b, 0, i)),
                   pl.BlockSpec((1, tm, 2 * A_DC), lambda b, i: (b, i, 0)),
                   pl.BlockSpec((1, IDX_DIM, tm), lambda b, i: (b, 0, i)),
                   pl.BlockSpec((1, IDX_HEADS, tm), lambda b, i: (b, 0, i))],
        out_shape=[jax.ShapeDtypeStruct((bsz, A_HEADS, t, A_DC), BF16),
                   jax.ShapeDtypeStruct((bsz, IDX_HEADS, t, IDX_DIM), BF16),
                   jax.ShapeDtypeStruct((bsz, A_DC, t), BF16),
                   jax.ShapeDtypeStruct((bsz, t, 2 * A_DC), BF16),
                   jax.ShapeDtypeStruct((bsz, IDX_DIM, t), BF16),
                   jax.ShapeDtypeStruct((bsz, IDX_HEADS, t), F32)],
        compiler_params=_params(("parallel", "parallel")),
        name="dsa_prep",
    )(a_in, cq_g.reshape(1, -1), ckv_g.reshape(1, -1), w_q)


def _a_strip_off():
    return REL_MAX_DIST + A_KEY_TILE


def _a_strip_len():
    return _a_strip_off() + A_KEY_TILE


def _a_strip_kernel(tbl_ref, o_ref):
    h = pl.program_id(0)
    shape = (Q_BLOCK, _a_strip_len())
    d = (lax.broadcasted_iota(I32, shape, 0) + _a_strip_off()
         - lax.broadcasted_iota(I32, shape, 1))
    bucket = _rel_bucket(d)
    far = tbl_ref[REL_BUCKETS - 1, h]
    val = jnp.zeros(shape, F32)
    for b in range(REL_BUCKETS - 1):
        val = jnp.where(bucket == b, (tbl_ref[b, h] - far) * LOG2E, val)
    o_ref[0] = val


def _a_strips(tbl):
    return pl.pallas_call(
        _a_strip_kernel,
        grid=(A_HEADS,),
        in_specs=[pl.BlockSpec(memory_space=pltpu.SMEM)],
        out_specs=pl.BlockSpec((1, Q_BLOCK, _a_strip_len()), lambda h: (h, 0, 0)),
        out_shape=jax.ShapeDtypeStruct((A_HEADS, Q_BLOCK, _a_strip_len()), F32),
        compiler_params=_params(("arbitrary",)),
        name="dsa_bias_strips",
    )(tbl)


def _c_bias_kernel(tbl_ref, o_ref, *, dilation, col0):
    v = pl.program_id(0)
    h = pl.program_id(1)
    shape = (Q_BLOCK, 2 * Q_BLOCK)
    i = lax.broadcasted_iota(I32, shape, 0)
    j = lax.broadcasted_iota(I32, shape, 1)
    dist = Q_BLOCK + i - j
    bucket = _rel_bucket(dist * dilation)
    val = jnp.zeros(shape, F32)
    for b in range(REL_BUCKETS):
        val = jnp.where(bucket == b, tbl_ref[b, col0 + h], val)
    valid = (dist >= 0) & (dist <= Q_BLOCK) & ((j >= Q_BLOCK) | (v > 0))
    o_ref[0, 0] = jnp.where(valid, val, NEG_BIG)


def _c_bias(tbl, dilation, col0):
    return pl.pallas_call(
        functools.partial(_c_bias_kernel, dilation=dilation, col0=col0),
        grid=(2, C_HEADS),
        in_specs=[pl.BlockSpec(memory_space=pltpu.SMEM)],
        out_specs=pl.BlockSpec((1, 1, Q_BLOCK, 2 * Q_BLOCK), lambda v, h: (v, h, 0, 0)),
        out_shape=jax.ShapeDtypeStruct((2, C_HEADS, Q_BLOCK, 2 * Q_BLOCK), F32),
        compiler_params=_params(("arbitrary", "arbitrary")),
        name="dilated_bias",
    )(tbl)


def _dsa_kernel(qi_ref, wi_ref, kit_ref, q_ref, kvt_ref, kva_ref, strip_ref, o_ref,
                sc_ref, thr_ref, kaug_ref, qa_ref, m_ref, acc_ref, lg_ref, p_ref, al_ref, *, topk, seq):
    qb = pl.program_id(1)
    hg = pl.program_id(2)
    w_tile = A_KEY_TILE
    q0 = qb * Q_BLOCK
    n_tiles = q0 // w_tile + 1
    n_pairs = (n_tiles + 1) // 2
    cols = w_tile // LANES
    rows = A_HEAD_GROUP * Q_BLOCK

    @pl.when((qb == 0) & (hg == 0))
    def _load_keys():
        kaug_ref[:A_DC, :] = kvt_ref[0]
    lane_iota = lax.broadcasted_iota(I32, (Q_BLOCK, w_tile), 1)
    row_iota = lax.broadcasted_iota(I32, (Q_BLOCK, w_tile), 0)

    def tile_start(kt):
        return pl.multiple_of(kt * w_tile, w_tile)

    @pl.when(hg == 0)
    def _select():
        qi = qi_ref[0].reshape(IDX_HEADS * Q_BLOCK, IDX_DIM)
        wcols = [wi_ref[0, h] for h in range(IDX_HEADS)]

        def score_tile(kt, carry):
            k0 = tile_start(kt)
            r = jnp.dot(qi, kit_ref[0, :, pl.ds(k0, w_tile)], preferred_element_type=F32)
            r = jnp.maximum(r, 0.0).reshape(IDX_HEADS, Q_BLOCK, w_tile)
            s = r[0] * wcols[0]
            for h in range(1, IDX_HEADS):
                s = s + r[h] * wcols[h]
            s = jnp.where(s == 0.0, 0.0, s)
            s = jnp.where(k0 + lane_iota <= q0 + row_iota, s, -jnp.inf)
            bits = pltpu.bitcast(s, I32)
            sc_ref[:, pl.ds(k0, w_tile)] = bits ^ ((bits >> 31) & 0x7FFFFFFF)
            return carry

        lax.fori_loop(0, n_tiles, score_tile, 0)

        def count(pred):
            def body(kt, cnt):
                k0 = tile_start(kt)
                blk = sc_ref[:, pl.ds(k0, w_tile)]
                for c in range(cols):
                    hit = pred(blk[:, c * LANES:(c + 1) * LANES], k0 + c * LANES)
                    cnt = cnt + jnp.where(hit, 1, 0)
                return cnt
            cnt = lax.fori_loop(0, n_tiles, body, jnp.zeros((Q_BLOCK, LANES), I32))
            return jnp.sum(cnt, axis=1, keepdims=True)

        def bcast(v):
            return jnp.broadcast_to(v, (Q_BLOCK, LANES))

        def bounds_tile(kt, carry):
            k0 = tile_start(kt)
            blk = sc_ref[:, pl.ds(k0, w_tile)]
            best = list(carry)
            for c in range(cols):
                best[c % 2] = jnp.maximum(best[c % 2], blk[:, c * LANES:(c + 1) * LANES])
            return tuple(best)

        floor = jnp.full((Q_BLOCK, LANES), INT_MIN, I32)
        best_even, best_odd = lax.fori_loop(0, n_tiles, bounds_tile, (floor, floor))

        def as_float(key):
            return pltpu.bitcast(key ^ ((key >> 31) & 0x7FFFFFFF), F32)

        def as_key(val):
            bits = pltpu.bitcast(val, I32)
            return bits ^ ((bits >> 31) & 0x7FFFFFFF)

        lo_key = as_key(jnp.broadcast_to(
            jnp.min(as_float(jnp.minimum(best_even, best_odd)), axis=1, keepdims=True), (Q_BLOCK, LANES)))
        hi_key = as_key(jnp.broadcast_to(
            jnp.max(as_float(jnp.maximum(best_even, best_odd)), axis=1, keepdims=True), (Q_BLOCK, LANES)))
        if topk <= 2 * LANES:
            lo_u = (lo_key ^ INT_MIN)[:, 0:1]
            span = (hi_key ^ INT_MIN)[:, 0:1] - lo_u
        else:
            lo_u = jnp.zeros((Q_BLOCK, 1), I32)
            span = jnp.full((Q_BLOCK, 1), -1, I32)
        n_bits = jnp.max((32 - lax.clz(span)).astype(F32)).astype(I32)

        def offset_bit(i, off):
            cand = off | lax.shift_left(jnp.int32(1), n_bits - 1 - i)
            in_span = (cand ^ INT_MIN) <= (span ^ INT_MIN)
            cand_b = bcast((lo_u + cand) ^ INT_MIN)
            c = count(lambda blk, _: blk >= cand_b)
            return jnp.where(in_span, jnp.where(c >= topk, cand, off), off)

        t = lo_u + lax.fori_loop(0, n_bits, offset_bit, jnp.zeros((Q_BLOCK, 1), I32))
        thr = jnp.maximum(t ^ INT_MIN, KEY_NEG_INF + 1)
        thr_b = bcast(thr)
        thr_ref[:, 0:1] = thr
        n_gt = count(lambda blk, _: blk > thr_b)
        n_eq = count(lambda blk, _: blk == thr_b)
        need = topk - n_gt
        thr_ref[:, 1:2] = jnp.full((Q_BLOCK, 1), 2 * seq, I32)

        @pl.when(jnp.max(n_eq - need) > 0)
        def _ties():
            lane = lax.broadcasted_iota(I32, (Q_BLOCK, LANES), 1)
            nbits = (2 * seq - 1).bit_length()

            def lim_bit(i, lim):
                cand = lim | lax.shift_left(jnp.int32(1), nbits - 1 - i)
                cand_b = bcast(cand)
                c = count(lambda blk, k0: jnp.where(blk == thr_b, k0 + lane, 2 * seq) < cand_b)
                return jnp.where(c <= need, cand, lim)

            thr_ref[:, 1:2] = lax.fori_loop(0, nbits, lim_bit, jnp.zeros((Q_BLOCK, 1), I32))

        lim = thr_ref[:, 1:2]

        def mask_tile(kt, carry):
            k0 = tile_start(kt)
            blk = sc_ref[:, pl.ds(k0, w_tile)]
            bound = jnp.where(k0 + lane_iota < lim, thr, thr + 1)
            add = jnp.where(blk >= bound, 0.0, NEG_BIG).astype(F32)
            add = jnp.where(kt < n_tiles, add, NEG_BIG)
            kaug_ref[A_DC:, pl.ds(k0, w_tile)] = add.astype(BF16)
            return carry

        lax.fori_loop(0, 2 * n_pairs, mask_tile, 0)

    chunk = 32
    eye = (lax.broadcasted_iota(I32, (Q_BLOCK, Q_BLOCK), 0)
           == lax.broadcasted_iota(I32, (Q_BLOCK, Q_BLOCK), 1))
    onehot = jnp.where(eye, 1.0, 0.0).astype(BF16)
    for hh in range(A_HEAD_GROUP):
        qa_ref[hh * Q_BLOCK:(hh + 1) * Q_BLOCK, :A_DC] = q_ref[0, hh]
        qa_ref[hh * Q_BLOCK:(hh + 1) * Q_BLOCK, A_DC:] = onehot
    m_ref[...] = jnp.full(m_ref.shape, NEG_BIG, F32)
    acc_ref[...] = jnp.zeros_like(acc_ref)

    def stage_a(kt, slot):
        k0 = tile_start(kt)
        lg_ref[slot] = jnp.dot(qa_ref[...], kaug_ref[:, pl.ds(k0, w_tile)], preferred_element_type=F32)

    def stage_b(kt, slot, near):
        k0 = tile_start(kt)
        l0 = pl.multiple_of(jnp.clip(_a_strip_off() - (q0 - k0), 0, _a_strip_off()), LANES)
        for hh in range(A_HEAD_GROUP):
            for r0 in range(0, Q_BLOCK, chunk):
                rs = pl.ds(hh * Q_BLOCK + r0, chunk)
                x = lg_ref[slot, rs, :]
                if near:
                    x = x + strip_ref[hh, pl.ds(r0, chunk), pl.ds(l0, w_tile)]
                m_old = m_ref[rs, :]
                m_new = jnp.maximum(m_old, jnp.max(x, axis=1, keepdims=True))
                for c in range(cols):
                    cs = slice(c * LANES, (c + 1) * LANES)
                    p_ref[slot, rs, cs] = jnp.exp2(x[:, cs] - m_new).astype(BF16)
                al_ref[slot, rs, :] = jnp.exp2(m_old - m_new)
                m_ref[rs, :] = m_new

    def stage_c(kt, slot):
        k0 = tile_start(kt)
        pv = jnp.dot(p_ref[slot], kva_ref[0, pl.ds(k0, w_tile), :], preferred_element_type=F32)
        al = al_ref[slot]
        acc_ref[:, :A_DC] = acc_ref[:, :A_DC] * al + pv[:, :A_DC]
        acc_ref[:, A_DC:] = acc_ref[:, A_DC:] * al + pv[:, A_DC:]

    stage_a(0, 0)
    stage_a(1, 1)
    stage_b(0, 0, True)

    def pair_body(i, near):
        kt = 2 * i
        stage_a(kt, 0)
        stage_b(kt - 1, 1, near)
        stage_c(kt - 2, 0)
        stage_a(kt + 1, 1)
        stage_b(kt, 0, near)
        stage_c(kt - 1, 1)

    def run_pairs(lo, hi, near):
        def two_pairs(j, carry):
            pair_body(lo + 2 * j, near)
            pair_body(lo + 2 * j + 1, near)
            return carry

        lax.fori_loop(0, (hi - lo) // 2, two_pairs, 0)

        @pl.when((hi - lo) % 2 == 1)
        def _():
            pair_body(hi - 1, near)

    n_far = jnp.maximum(q0 - (REL_MAX_DIST - 1), 0) // w_tile
    first_near = jnp.maximum((n_far + 1) // 2, 1)
    run_pairs(1, first_near, False)
    run_pairs(first_near, n_pairs, True)
    last = 2 * n_pairs - 1
    stage_b(last, 1, True)
    stage_c(last - 1, 0)
    stage_c(last, 1)

    acc = acc_ref[...]
    out = acc[:, :A_DC] * (1.0 / acc[:, A_DC:A_DC + 1])
    for hh in range(A_HEAD_GROUP):
        o_ref[0, :, hh * A_DC:(hh + 1) * A_DC] = out[hh * Q_BLOCK:(hh + 1) * Q_BLOCK].astype(o_ref.dtype)


def _dsa_attention(qi, wi, kit, q, kvt, kva, strips, topk):
    bsz, _, t, _ = q.shape
    assert t % (2 * A_KEY_TILE) == 0
    groups = A_HEADS // A_HEAD_GROUP
    rows = A_HEAD_GROUP * Q_BLOCK
    resident = dict(pipeline_mode=pl.Buffered(1))
    return pl.pallas_call(
        functools.partial(_dsa_kernel, topk=topk, seq=t),
        grid=(bsz, t // Q_BLOCK, groups),
        in_specs=[
            pl.BlockSpec((1, IDX_HEADS, Q_BLOCK, IDX_DIM), lambda b, i, g: (b, 0, i, 0)),
            pl.BlockSpec((1, IDX_HEADS, Q_BLOCK, 1), lambda b, i, g: (b, 0, i, 0)),
            pl.BlockSpec((1, IDX_DIM, t), lambda b, i, g: (b, 0, 0), **resident),
            pl.BlockSpec((1, A_HEAD_GROUP, Q_BLOCK, A_DC), lambda b, i, g: (b, g, i, 0)),
            pl.BlockSpec((1, A_DC, t), lambda b, i, g: (b, 0, 0), **resident),
            pl.BlockSpec((1, t, 2 * A_DC), lambda b, i, g: (b, 0, 0), **resident),
            pl.BlockSpec((A_HEAD_GROUP, Q_BLOCK, _a_strip_len()), lambda b, i, g: (g, 0, 0)),
        ],
        out_specs=pl.BlockSpec((1, Q_BLOCK, A_HEAD_GROUP * A_DC), lambda b, i, g: (b, i, g)),
        out_shape=jax.ShapeDtypeStruct((bsz, t, A_HEADS * A_DC), BF16),
        scratch_shapes=[pltpu.VMEM((Q_BLOCK, t), I32),
                        pltpu.VMEM((Q_BLOCK, LANES), I32),
                        pltpu.VMEM((2 * A_DC, t), BF16),
                        pltpu.VMEM((rows, 2 * A_DC), BF16),
                        pltpu.VMEM((rows, LANES), F32),
                        pltpu.VMEM((rows, 2 * A_DC), F32),
                        pltpu.VMEM((2, rows, A_KEY_TILE), F32),
                        pltpu.VMEM((2, rows, A_KEY_TILE), BF16),
                        pltpu.VMEM((2, rows, LANES), F32)],
        compiler_params=_params(("parallel", "arbitrary", "arbitrary")),
        name="dsa_attention",
    )(qi, wi, kit, q, kvt, kva, strips)


def _dsa_mixer(a_in, cq_g, ckv_g, w_uq, w_iq, strips, bsz, t):
    w_q = jnp.concatenate([w_uq, w_iq], axis=1).astype(BF16)
    ql, qi, kvt, kva, kit, wit = _dsa_prep(a_in.reshape(bsz, t, -1), cq_g, ckv_g, w_q)
    topk = min(TOPK_MAX, t // 4)
    return _dsa_attention(qi, wit[..., None], kit, ql, kvt, kva, strips, topk)


def _split3(x):
    hi = x.astype(BF16)
    r1 = x - hi.astype(F32)
    mid = r1.astype(BF16)
    lo = (r1 - mid.astype(F32)).astype(BF16)
    return hi, mid, lo


def _ssd_kernel(z_ref, xbc_ref, dt_ref, cw_ref, cb_ref, dtb_ref, alog_ref, dsk_ref, ng_ref, y_ref,
                ext_ref, st_ref, yb_ref, *, d_inner):
    c = pl.program_id(1)
    q = SSM_CHUNK
    n_state = SSM_STATE
    p_dim = SSM_HEADDIM
    hpg = d_inner // p_dim // SSM_GROUPS
    hist = 8

    @pl.when(c == 0)
    def _():
        ext_ref[0:hist, :] = jnp.zeros((hist, ext_ref.shape[1]), F32)
        st_ref[...] = jnp.zeros_like(st_ref)

    ext_ref[hist:hist + q, :] = xbc_ref[0]
    conv = cb_ref[...]
    for k in range(SSM_CONV):
        lo = hist - (SSM_CONV - 1) + k
        conv = conv + cw_ref[k:k + 1, :] * ext_ref[lo:lo + q, :]
    ext_ref[0:hist, :] = ext_ref[q:q + hist, :]
    u = conv * _sigmoid(conv)
    gn = SSM_GROUPS * n_state
    xs = u[:, :d_inner]
    bm = u[:, d_inner:d_inner + gn]
    cm = u[:, d_inner + gn:]

    raw = dt_ref[0] + dtb_ref[...]
    dt = jnp.maximum(raw, 0.0) + jnp.log(1.0 + jnp.exp(-jnp.abs(raw)))
    dta = dt * (-jnp.exp(alog_ref[...]))
    ii = lax.broadcasted_iota(I32, (q, q), 0)
    jj = lax.broadcasted_iota(I32, (q, q), 1)
    tril = ii >= jj
    tri = jnp.where(tril, 1.0, 0.0).astype(BF16)
    cum = sum(jnp.dot(tri, part, preferred_element_type=F32) for part in _split3(dta))
    cum_t = cum.T
    dt_t = dt.T
    last_t = cum_t[:, q - 1:q]
    ecum = jnp.exp(cum)
    wst_t = jnp.exp(last_t - cum_t) * dt_t
    elast = jnp.exp(last_t)

    for g in range(SSM_GROUPS):
        bc = bm[:, g * n_state:(g + 1) * n_state]
        cc = cm[:, g * n_state:(g + 1) * n_state]
        bc16 = bc.astype(BF16)
        cc16 = cc.astype(BF16)
        cb = lax.dot_general(cc16, bc16, (((1,), (1,)), ((), ())), preferred_element_type=F32)
        xs_t = xs[:, g * hpg * p_dim:(g + 1) * hpg * p_dim].T
        for hh in range(hpg):
            h = g * hpg + hh
            seg = cum[:, h:h + 1] - cum_t[h:h + 1, :]
            decay = jnp.exp(jnp.where(tril, seg, -jnp.inf))
            wgt = (decay * cb * dt_t[h:h + 1, :]).astype(BF16)
            xc16 = xs[:, h * p_dim:(h + 1) * p_dim].astype(BF16)
            st = st_ref[h]
            y_off = lax.dot_general(cc16, st.astype(BF16), (((1,), (1,)), ((), ())),
                                    preferred_element_type=F32)
            y = jnp.dot(wgt, xc16, preferred_element_type=F32) + y_off * ecum[:, h:h + 1]
            wx_t = (xs_t[hh * p_dim:(hh + 1) * p_dim, :] * wst_t[h:h + 1, :]).astype(BF16)
            st_ref[h] = st * elast[h:h + 1, :] + jnp.dot(wx_t, bc16, preferred_element_type=F32)
            yb_ref[:, h * p_dim:(h + 1) * p_dim] = y

    z = z_ref[0]
    y = (yb_ref[...] + xs * dsk_ref[...]) * (z * _sigmoid(z))
    gw = hpg * p_dim
    for g in range(SSM_GROUPS):
        yg = y[:, g * gw:(g + 1) * gw]
        yn = yg * lax.rsqrt(jnp.mean(yg * yg, axis=-1, keepdims=True) + EPS)
        y_ref[0, :, g * gw:(g + 1) * gw] = (yn * ng_ref[:, g * gw:(g + 1) * gw]).astype(y_ref.dtype)


def _ssd_mixer(z, xbc, dt, conv_w, conv_b, dt_bias, a_log, d_skip, norm_g):
    bsz, t, d_inner = z.shape
    conv_dim = xbc.shape[-1]
    heads = d_inner // SSM_HEADDIM
    pad = LANES - heads
    dtb = jnp.pad(dt_bias, (0, pad)).reshape(1, LANES)
    alog = jnp.pad(a_log, (0, pad)).reshape(1, LANES)
    dsk = jnp.repeat(d_skip, SSM_HEADDIM).reshape(1, d_inner)
    fixed = lambda b, c: (0, 0)
    return pl.pallas_call(
        functools.partial(_ssd_kernel, d_inner=d_inner),
        grid=(bsz, t // SSM_CHUNK),
        in_specs=[pl.BlockSpec((1, SSM_CHUNK, d_inner), lambda b, c: (b, c, 0)),
                  pl.BlockSpec((1, SSM_CHUNK, conv_dim), lambda b, c: (b, c, 0)),
                  pl.BlockSpec((1, SSM_CHUNK, LANES), lambda b, c: (b, c, 0)),
                  pl.BlockSpec((SSM_CONV, conv_dim), fixed),
                  pl.BlockSpec((1, conv_dim), fixed),
                  pl.BlockSpec((1, LANES), fixed),
                  pl.BlockSpec((1, LANES), fixed),
                  pl.BlockSpec((1, d_inner), fixed),
                  pl.BlockSpec((1, d_inner), fixed)],
        out_specs=pl.BlockSpec((1, SSM_CHUNK, d_inner), lambda b, c: (b, c, 0)),
        out_shape=jax.ShapeDtypeStruct((bsz, t, d_inner), BF16),
        scratch_shapes=[pltpu.VMEM((SSM_CHUNK + 8, conv_dim), F32),
                        pltpu.VMEM((heads, SSM_HEADDIM, SSM_STATE), F32),
                        pltpu.VMEM((SSM_CHUNK, d_inner), F32)],
        compiler_params=_params(("parallel", "arbitrary")),
        name="ssd",
    )(z, xbc, dt, conv_w.reshape(SSM_CONV, conv_dim), conv_b.reshape(1, conv_dim), dtb, alog, dsk,
      norm_g.reshape(1, d_inner))


def _dilated_kernel(q_ref, kc_ref, kp_ref, vc_ref, vp_ref, b_ref, o_ref, l_ref):
    d = C_HEADDIM
    q = q_ref[0]
    k = jnp.concatenate([kp_ref[0], kc_ref[0]], axis=0)
    v = jnp.concatenate([vp_ref[0], vc_ref[0]], axis=0)
    for h in range(C_HEADS):
        sl = slice(h * d, (h + 1) * d)
        lg = lax.dot_general(q[:, sl], k[:, sl], (((1,), (1,)), ((), ())), preferred_element_type=F32)
        x = lg * d ** -0.5 + b_ref[0, h]
        m = jnp.max(x, axis=1, keepdims=True)
        p = jnp.exp(x - m)
        s = jnp.sum(p, axis=1, keepdims=True)
        o = jnp.dot(p.astype(BF16), v[:, sl], preferred_element_type=F32)
        o_ref[0, :, sl] = o * (1.0 / s)
        l_ref[0, :, sl] = jnp.broadcast_to(m + jnp.log(s), (Q_BLOCK, d))


def _dilated_group(qkv, bias, g, dilation):
    bsz, t, cols = qkv.shape
    hd = C_HEADS * C_HEADDIM
    per_tok = cols // hd
    tr = t // dilation
    view = qkv.reshape(bsz, tr, dilation * cols)
    col = lambda c, j: c * per_tok + g * 3 + j
    cur = lambda j: (lambda b, c, n: (b, n, col(c, j)))
    prev = lambda j: (lambda b, c, n: (b, jnp.maximum(n - 1, 0), col(c, j)))
    blk = (1, Q_BLOCK, hd)
    out_spec = pl.BlockSpec(blk, lambda b, c, n: (b, n, c))
    o, lse = pl.pallas_call(
        _dilated_kernel,
        grid=(bsz, dilation, tr // Q_BLOCK),
        in_specs=[pl.BlockSpec(blk, cur(0)), pl.BlockSpec(blk, cur(1)), pl.BlockSpec(blk, prev(1)),
                  pl.BlockSpec(blk, cur(2)), pl.BlockSpec(blk, prev(2)),
                  pl.BlockSpec((1, C_HEADS, Q_BLOCK, 2 * Q_BLOCK),
                               lambda b, c, n: (jnp.minimum(n, 1), 0, 0, 0))],
        out_specs=[out_spec, out_spec],
        out_shape=[jax.ShapeDtypeStruct((bsz, tr, dilation * hd), F32)] * 2,
        compiler_params=_params(("parallel", "parallel", "arbitrary")),
        name="dilated_attention",
    )(view, view, view, view, view, bias)
    return o.reshape(bsz * t, hd), lse.reshape(bsz * t, hd)


def _merge_kernel(x_ref, ya_ref, yb_ref, o1_ref, o2_ref, o3_ref, l1_ref, l2_ref, l3_ref,
                  ga_ref, gb_ref, gc_ref, wa_ref, wb_ref, wc_ref, wo_ref, out_ref):
    l1, l2, l3 = l1_ref[...], l2_ref[...], l3_ref[...]
    lm = jnp.maximum(jnp.maximum(l1, l2), l3)
    e1, e2, e3 = jnp.exp(l1 - lm), jnp.exp(l2 - lm), jnp.exp(l3 - lm)
    yc = (e1 * o1_ref[...] + e2 * o2_ref[...] + e3 * o3_ref[...]) * (1.0 / (e1 + e2 + e3))
    pa = jnp.dot(ya_ref[...], wa_ref[...], preferred_element_type=F32)
    pb = jnp.dot(yb_ref[...], wb_ref[...], preferred_element_type=F32)
    pc = jnp.dot(yc.astype(BF16), wc_ref[...], preferred_element_type=F32)
    merged = _sigmoid(ga_ref[...]) * pa + _sigmoid(gb_ref[...]) * pb + _sigmoid(gc_ref[...]) * pc
    out_ref[...] = x_ref[...] + jnp.dot(merged.astype(BF16), wo_ref[...], preferred_element_type=F32)


def _merge(x, ya, yb, oc, lc, gates, w_a, w_b, w_c, w_o, tm=256):
    m, d = x.shape
    row = lambda i: (i, 0)
    fixed = lambda i: (0, 0)
    full = lambda a: pl.BlockSpec(a.shape, fixed)
    rows = lambda a: pl.BlockSpec((tm, a.shape[1]), row)
    weights = [w_a.astype(BF16), w_b.astype(BF16), w_c.astype(BF16), w_o.astype(BF16)]
    return pl.pallas_call(
        _merge_kernel,
        grid=(m // tm,),
        in_specs=[rows(x), rows(ya), rows(yb)] + [rows(a) for a in oc] + [rows(a) for a in lc]
                 + [pl.BlockSpec((tm, d), lambda i, j=j: (i, j)) for j in range(3)]
                 + [full(w) for w in weights],
        out_specs=pl.BlockSpec((tm, d), row),
        out_shape=jax.ShapeDtypeStruct((m, d), F32),
        compiler_params=_params(("parallel",)),
        name="merge",
    )(x, ya, yb, *oc, *lc, gates, gates, gates, *weights)


def _pad_cols(w, n):
    return jnp.pad(w, ((0, 0), (0, n - w.shape[1])))


def kernel(x, p, rel_bias, final_norm, ffn1_norm, ffn1_gate, ffn1_up, ffn1_down, mix_norm, w_in,
           cq_norm, ckv_norm, w_uq, w_iq, conv_w, conv_b, dt_bias, a_log, d_skip, ssm_norm,
           w_out_a, w_out_b, w_out_c, w_o, ffn2_norm, ffn2_gate, ffn2_up, ffn2_down, ple_norm,
           w_ple_gate, w_ple_proj):
    bsz, t, d_model = x.shape
    depth = w_in.shape[0]
    m = bsz * t
    d_inner = ssm_norm.shape[1]
    conv_dim = conv_b.shape[1]
    ssm_heads = dt_bias.shape[1]
    c_cols = len(C_GROUPS) * 3 * C_HEADS * C_HEADDIM
    sizes = (A_DCQ, A_DC, IDX_DIM, IDX_HEADS, d_inner, conv_dim, ssm_heads, c_cols,
             d_model, d_model, d_model)
    offs = [0]
    for s in sizes:
        offs.append(offs[-1] + s)

    strips = _a_strips(rel_bias)
    c_bias = [_c_bias(rel_bias, dil, A_HEADS + g * C_HEADS) for g, (_, dil) in enumerate(C_GROUPS)]

    x = x.reshape(m, d_model)
    for i in range(depth):
        x = _ffn(x, ffn1_norm[i], ffn1_gate[i], ffn1_up[i], ffn1_down[i])

        w = w_in[i].astype(BF16)
        g_mix = mix_norm[i]
        a_in = _norm_matmul(x, g_mix, _pad_cols(w[:, offs[0]:offs[4]], 4 * LANES), F32, 512, 4 * LANES)
        z = _norm_matmul(x, g_mix, w[:, offs[4]:offs[5]], F32, 1024, 1024)
        xbc = _norm_matmul(x, g_mix, w[:, offs[5]:offs[6]], F32, 1024, 1024)
        dt = _norm_matmul(x, g_mix, _pad_cols(w[:, offs[6]:offs[7]], LANES), F32, 512, LANES)
        qkv = _norm_matmul(x, g_mix, w[:, offs[7]:offs[8]], BF16, 1024, 1536)
        gates = _norm_matmul(x, g_mix, w[:, offs[8]:offs[11]], F32, 1024, 1024)

        ya = _dsa_mixer(a_in, cq_norm[i], ckv_norm[i], w_uq[i], w_iq[i], strips, bsz, t)
        yb = _ssd_mixer(z.reshape(bsz, t, d_inner), xbc.reshape(bsz, t, conv_dim),
                        dt.reshape(bsz, t, LANES), conv_w[i], conv_b[i], dt_bias[i], a_log[i],
                        d_skip[i], ssm_norm[i])
        qkv = qkv.reshape(bsz, t, c_cols)
        oc, lc = zip(*[_dilated_group(qkv, c_bias[g], g, dil) for g, (_, dil) in enumerate(C_GROUPS)])
        x = _merge(x, ya.reshape(m, -1), yb.reshape(m, -1), oc, lc, gates,
                   w_out_a[i], w_out_b[i], w_out_c[i], w_o[i])

        x = _ffn(x, ffn2_norm[i], ffn2_gate[i], ffn2_up[i], ffn2_down[i])
        x = _ple(x, p[i].reshape(m, -1), ple_norm[i], w_ple_gate[i], w_ple_proj[i], final_norm,
                 final_norm=(i == depth - 1))
    return x.reshape(bsz, t, d_model)
```

```python
import functools
import math

import jax
import jax.numpy as jnp
from jax import lax
from jax.experimental import pallas as pl
from jax.experimental.pallas import tpu as pltpu

F32 = jnp.float32
BF16 = jnp.bfloat16
I32 = jnp.int32

EPS = 1e-6
LANES = 128
Q_BLOCK = 128
VMEM_LIMIT = 56 * 1024 * 1024

A_HEADS = 16
A_DC = 128
A_DCQ = 256
IDX_HEADS = 8
IDX_DIM = 64
TOPK_MAX = 256
A_HEAD_GROUP = 4
A_KEY_TILE = 512
SSM_HEADDIM = 64
SSM_GROUPS = 8
SSM_STATE = 128
SSM_CONV = 4
SSM_CHUNK = 128
C_GROUPS = ((128, 1), (512, 4), (2048, 16))
C_HEADS = 8
C_HEADDIM = 64
REL_BUCKETS = 32
REL_MAX_DIST = 2048

LOG2E = 1.4426950408889634
NEG_BIG = -1e30
KEY_NEG_INF = -2139095041
INT_MIN = -2147483648


def _params(sem, vmem=VMEM_LIMIT):
    return pltpu.CompilerParams(dimension_semantics=sem, vmem_limit_bytes=vmem)


def _rms(x, g):
    return x * lax.rsqrt(jnp.mean(x * x, axis=-1, keepdims=True) + EPS) * g


def _sigmoid(x):
    return 1.0 / (1.0 + jnp.exp(-x))


def _rel_bucket(d):
    max_exact = REL_BUCKETS // 2
    d = jnp.maximum(d, 0)
    df = jnp.maximum(d, 1).astype(F32)
    large = max_exact + (jnp.log(df / max_exact) / math.log(REL_MAX_DIST / max_exact)
                         * (REL_BUCKETS - max_exact)).astype(I32)
    large = jnp.minimum(large, REL_BUCKETS - 1)
    return jnp.where(d < max_exact, d, large)


def _norm_matmul_kernel(x_ref, g_ref, w_ref, o_ref, hn_ref):
    @pl.when(pl.program_id(1) == 0)
    def _():
        hn_ref[...] = _rms(x_ref[...], g_ref[...]).astype(BF16)

    o_ref[...] = jnp.dot(hn_ref[...], w_ref[...], preferred_element_type=F32).astype(o_ref.dtype)


def _norm_matmul(x, g, w, out_dtype, tm, tn):
    m, k = x.shape
    n = w.shape[1]
    return pl.pallas_call(
        _norm_matmul_kernel,
        grid=(m // tm, n // tn),
        in_specs=[pl.BlockSpec((tm, k), lambda i, j: (i, 0)),
                  pl.BlockSpec((1, k), lambda i, j: (0, 0)),
                  pl.BlockSpec((k, tn), lambda i, j: (0, j))],
        out_specs=pl.BlockSpec((tm, tn), lambda i, j: (i, j)),
        out_shape=jax.ShapeDtypeStruct((m, n), out_dtype),
        scratch_shapes=[pltpu.VMEM((tm, k), BF16)],
        compiler_params=_params(("parallel", "arbitrary")),
        name="norm_matmul",
    )(x, g.reshape(1, k), w)


def _ffn_kernel(x_ref, g_ref, wg_ref, wu_ref, wd_ref, o_ref, hn_ref, acc_ref):
    j = pl.program_id(1)

    @pl.when(j == 0)
    def _():
        hn_ref[...] = _rms(x_ref[...], g_ref[...]).astype(BF16)
        acc_ref[...] = jnp.zeros_like(acc_ref)

    h = hn_ref[...]
    gate = jnp.dot(h, wg_ref[...], preferred_element_type=F32)
    up = jnp.dot(h, wu_ref[...], preferred_element_type=F32)
    act = (gate * _sigmoid(gate) * up).astype(BF16)
    acc_ref[...] += jnp.dot(act, wd_ref[...], preferred_element_type=F32)

    @pl.when(j == pl.num_programs(1) - 1)
    def _():
        o_ref[...] = x_ref[...] + 0.5 * acc_ref[...]


def _ffn(x, g, w_gate, w_up, w_down, tm=512):
    m, d = x.shape
    f = w_gate.shape[1]
    tf = f // 2 if (f // 2) % LANES == 0 else f
    return pl.pallas_call(
        _ffn_kernel,
        grid=(m // tm, f // tf),
        in_specs=[pl.BlockSpec((tm, d), lambda i, j: (i, 0)),
                  pl.BlockSpec((1, d), lambda i, j: (0, 0)),
                  pl.BlockSpec((d, tf), lambda i, j: (0, j)),
                  pl.BlockSpec((d, tf), lambda i, j: (0, j)),
                  pl.BlockSpec((tf, d), lambda i, j: (j, 0))],
        out_specs=pl.BlockSpec((tm, d), lambda i, j: (i, 0)),
        out_shape=jax.ShapeDtypeStruct((m, d), F32),
        scratch_shapes=[pltpu.VMEM((tm, d), BF16), pltpu.VMEM((tm, d), F32)],
        compiler_params=_params(("parallel", "arbitrary")),
        name="ffn",
    )(x, g.reshape(1, d), w_gate.astype(BF16), w_up.astype(BF16), w_down.astype(BF16))


def _ple_kernel(x_ref, p_ref, g_ref, wg_ref, wp_ref, fg_ref, o_ref, *, final_norm):
    x = x_ref[...]
    h = _rms(x, g_ref[...]).astype(BF16)
    gate = _sigmoid(jnp.dot(h, wg_ref[...], preferred_element_type=F32))
    proj = jnp.dot(p_ref[...].astype(BF16), wp_ref[...], preferred_element_type=F32)
    y = x + gate * proj
    if final_norm:
        y = _rms(y, fg_ref[...])
    o_ref[...] = y


def _ple(x, p, g, w_gate, w_proj, final_g, final_norm, tm=512):
    m, d = x.shape
    dp = p.shape[1]
    return pl.pallas_call(
        functools.partial(_ple_kernel, final_norm=final_norm),
        grid=(m // tm,),
        in_specs=[pl.BlockSpec((tm, d), lambda i: (i, 0)),
                  pl.BlockSpec((tm, dp), lambda i: (i, 0)),
                  pl.BlockSpec((1, d), lambda i: (0, 0)),
                  pl.BlockSpec((d, d), lambda i: (0, 0)),
                  pl.BlockSpec((dp, d), lambda i: (0, 0)),
                  pl.BlockSpec((1, d), lambda i: (0, 0))],
        out_specs=pl.BlockSpec((tm, d), lambda i: (i, 0)),
        out_shape=jax.ShapeDtypeStruct((m, d), F32),
        compiler_params=_params(("parallel",)),
        name="ple",
    )(x, p, g.reshape(1, d), w_gate.astype(BF16), w_proj.astype(BF16), final_g.reshape(1, d))


def _dsa_prep_kernel(a_ref, cqg_ref, ckvg_ref, wq_ref, ql_ref, qi_ref, kvt_ref, kva_ref, kit_ref,
                     wit_ref):
    a = a_ref[0]
    tm = a.shape[0]
    cqn = _rms(a[:, :A_DCQ], cqg_ref[...]).astype(BF16)
    q = jnp.dot(cqn, wq_ref[...], preferred_element_type=F32)
    nl = A_HEADS * A_DC
    ql = (q[:, :nl] * (A_DC ** -0.5 * LOG2E)).astype(BF16)
    for h in range(A_HEADS):
        ql_ref[0, h] = ql[:, h * A_DC:(h + 1) * A_DC]
    qi = q[:, nl:].astype(BF16)
    for h in range(IDX_HEADS):
        qi_ref[0, h] = qi[:, h * IDX_DIM:(h + 1) * IDX_DIM]
    kv = _rms(a[:, A_DCQ:A_DCQ + A_DC], ckvg_ref[...])
    kvt_ref[0] = kv.T.astype(BF16)
    lane = lax.broadcasted_iota(I32, (tm, A_DC), 1)
    kva_ref[0, :, :A_DC] = kv.astype(BF16)
    kva_ref[0, :, A_DC:] = jnp.where(lane == 0, 1.0, 0.0).astype(BF16)
    o = A_DCQ + A_DC
    rest_t = a[:, o:o + LANES].T
    kit_ref[0] = rest_t[:IDX_DIM].astype(BF16)
    wit_ref[0] = rest_t[IDX_DIM:IDX_DIM + IDX_HEADS] * (IDX_HEADS * IDX_DIM) ** -0.5


def _dsa_prep(a_in, cq_g, ckv_g, w_q, tm=512):
    bsz, t, na = a_in.shape
    nq = w_q.shape[1]
    fixed = lambda b, i: (0, 0)
    return pl.pallas_call(
        _dsa_prep_kernel,
        grid=(bsz, t // tm),
        in_specs=[pl.BlockSpec((1, tm, na), lambda b, i: (b, i, 0)),
                  pl.BlockSpec((1, A_DCQ), fixed),
                  pl.BlockSpec((1, A_DC), fixed),
                  pl.BlockSpec((A_DCQ, nq), fixed)],
        out_specs=[pl.BlockSpec((1, A_HEADS, tm, A_DC), lambda b, i: (b, 0, i, 0)),
                   pl.BlockSpec((1, IDX_HEADS, tm, IDX_DIM), lambda b, i: (b, 0, i, 0)),
                   pl.BlockSpec((1, A_DC, tm), lambda b, i: (b, 0, i)),
                   pl.BlockSpec((1, tm, 2 * A_DC), lambda b, i: (b, i, 0)),
                   pl.BlockSpec((1, IDX_DIM, tm), lambda b, i: (b, 0, i)),
                   pl.BlockSpec((1, IDX_HEADS, tm), lambda b, i: (b, 0, i))],
        out_shape=[jax.ShapeDtypeStruct((bsz, A_HEADS, t, A_DC), BF16),
                   jax.ShapeDtypeStruct((bsz, IDX_HEADS, t, IDX_DIM), BF16),
                   jax.ShapeDtypeStruct((bsz, A_DC, t), BF16),
                   jax.ShapeDtypeStruct((bsz, t, 2 * A_DC), BF16),
                   jax.ShapeDtypeStruct((bsz, IDX_DIM, t), BF16),
                   jax.ShapeDtypeStruct((bsz, IDX_HEADS, t), F32)],
        compiler_params=_params(("parallel", "parallel")),
        name="dsa_prep",
    )(a_in, cq_g.reshape(1, -1), ckv_g.reshape(1, -1), w_q)


def _a_strip_off():
    return REL_MAX_DIST + A_KEY_TILE


def _a_strip_len():
    return _a_strip_off() + A_KEY_TILE


def _a_strip_kernel(tbl_ref, o_ref):
    h = pl.program_id(0)
    shape = (Q_BLOCK, _a_strip_len())
    d = (lax.broadcasted_iota(I32, shape, 0) + _a_strip_off()
         - lax.broadcasted_iota(I32, shape, 1))
    bucket = _rel_bucket(d)
    far = tbl_ref[REL_BUCKETS - 1, h]
    val = jnp.zeros(shape, F32)
    for b in range(REL_BUCKETS - 1):
        val = jnp.where(bucket == b, (tbl_ref[b, h] - far) * LOG2E, val)
    o_ref[0] = val


def _a_strips(tbl):
    return pl.pallas_call(
        _a_strip_kernel,
        grid=(A_HEADS,),
        in_specs=[pl.BlockSpec(memory_space=pltpu.SMEM)],
        out_specs=pl.BlockSpec((1, Q_BLOCK, _a_strip_len()), lambda h: (h, 0, 0)),
        out_shape=jax.ShapeDtypeStruct((A_HEADS, Q_BLOCK, _a_strip_len()), F32),
        compiler_params=_params(("arbitrary",)),
        name="dsa_bias_strips",
    )(tbl)


def _c_bias_kernel(tbl_ref, o_ref, *, dilation, col0):
    v = pl.program_id(0)
    h = pl.program_id(1)
    shape = (Q_BLOCK, 2 * Q_BLOCK)
    i = lax.broadcasted_iota(I32, shape, 0)
    j = lax.broadcasted_iota(I32, shape, 1)
    dist = Q_BLOCK + i - j
    bucket = _rel_bucket(dist * dilation)
    val = jnp.zeros(shape, F32)
    for b in range(REL_BUCKETS):
        val = jnp.where(bucket == b, tbl_ref[b, col0 + h], val)
    valid = (dist >= 0) & (dist <= Q_BLOCK) & ((j >= Q_BLOCK) | (v > 0))
    o_ref[0, 0] = jnp.where(valid, val, NEG_BIG)


def _c_bias(tbl, dilation, col0):
    return pl.pallas_call(
        functools.partial(_c_bias_kernel, dilation=dilation, col0=col0),
        grid=(2, C_HEADS),
        in_specs=[pl.BlockSpec(memory_space=pltpu.SMEM)],
        out_specs=pl.BlockSpec((1, 1, Q_BLOCK, 2 * Q_BLOCK), lambda v, h: (v, h, 0, 0)),
        out_shape=jax.ShapeDtypeStruct((2, C_HEADS, Q_BLOCK, 2 * Q_BLOCK), F32),
        compiler_params=_params(("arbitrary", "arbitrary")),
        name="dilated_bias",
    )(tbl)


def _dsa_kernel(qi_ref, wi_ref, kit_ref, q_ref, kvt_ref, kva_ref, strip_ref, o_ref,
                sc_ref, thr_ref, kaug_ref, qa_ref, m_ref, acc_ref, lg_ref, p_ref, al_ref, *, topk, seq):
    qb = pl.program_id(1)
    hg = pl.program_id(2)
    w_tile = A_KEY_TILE
    q0 = qb * Q_BLOCK
    n_tiles = q0 // w_tile + 1
    n_pairs = (n_tiles + 1) // 2
    cols = w_tile // LANES
    rows = A_HEAD_GROUP * Q_BLOCK

    @pl.when((qb == 0) & (hg == 0))
    def _load_keys():
        kaug_ref[:A_DC, :] = kvt_ref[0]
    lane_iota = lax.broadcasted_iota(I32, (Q_BLOCK, w_tile), 1)
    row_iota = lax.broadcasted_iota(I32, (Q_BLOCK, w_tile), 0)

    def tile_start(kt):
        return pl.multiple_of(kt * w_tile, w_tile)

    @pl.when(hg == 0)
    def _select():
        qi = qi_ref[0].reshape(IDX_HEADS * Q_BLOCK, IDX_DIM)
        wcols = [wi_ref[0, h] for h in range(IDX_HEADS)]

        def score_tile(kt, carry):
            k0 = tile_start(kt)
            r = jnp.dot(qi, kit_ref[0, :, pl.ds(k0, w_tile)], preferred_element_type=F32)
            r = jnp.maximum(r, 0.0).reshape(IDX_HEADS, Q_BLOCK, w_tile)
            s = r[0] * wcols[0]
            for h in range(1, IDX_HEADS):
                s = s + r[h] * wcols[h]
            s = jnp.where(s == 0.0, 0.0, s)
            s = jnp.where(k0 + lane_iota <= q0 + row_iota, s, -jnp.inf)
            bits = pltpu.bitcast(s, I32)
            sc_ref[:, pl.ds(k0, w_tile)] = bits ^ ((bits >> 31) & 0x7FFFFFFF)
            return carry

        lax.fori_loop(0, n_tiles, score_tile, 0)

        def count(pred):
            def body(kt, cnt):
                k0 = tile_start(kt)
                blk = sc_ref[:, pl.ds(k0, w_tile)]
                for c in range(cols):
                    hit = pred(blk[:, c * LANES:(c + 1) * LANES], k0 + c * LANES)
                    cnt = cnt + jnp.where(hit, 1, 0)
                return cnt
            cnt = lax.fori_loop(0, n_tiles, body, jnp.zeros((Q_BLOCK, LANES), I32))
            return jnp.sum(cnt, axis=1, keepdims=True)

        def bcast(v):
            return jnp.broadcast_to(v, (Q_BLOCK, LANES))

        def bounds_tile(kt, carry):
            k0 = tile_start(kt)
            blk = sc_ref[:, pl.ds(k0, w_tile)]
            best = list(carry)
            for c in range(cols):
                best[c % 2] = jnp.maximum(best[c % 2], blk[:, c * LANES:(c + 1) * LANES])
            return tuple(best)

        floor = jnp.full((Q_BLOCK, LANES), INT_MIN, I32)
        best_even, best_odd = lax.fori_loop(0, n_tiles, bounds_tile, (floor, floor))

        def as_float(key):
            return pltpu.bitcast(key ^ ((key >> 31) & 0x7FFFFFFF), F32)

        def as_key(val):
            bits = pltpu.bitcast(val, I32)
            return bits ^ ((bits >> 31) & 0x7FFFFFFF)

        lo_key = as_key(jnp.broadcast_to(
            jnp.min(as_float(jnp.minimum(best_even, best_odd)), axis=1, keepdims=True), (Q_BLOCK, LANES)))
        hi_key = as_key(jnp.broadcast_to(
            jnp.max(as_float(jnp.maximum(best_even, best_odd)), axis=1, keepdims=True), (Q_BLOCK, LANES)))
        if topk <= 2 * LANES:
            lo_u = (lo_key ^ INT_MIN)[:, 0:1]
            span = (hi_key ^ INT_MIN)[:, 0:1] - lo_u
        else:
            lo_u = jnp.zeros((Q_BLOCK, 1), I32)
            span = jnp.full((Q_BLOCK, 1), -1, I32)
        n_bits = jnp.max((32 - lax.clz(span)).astype(F32)).astype(I32)

        def offset_bit(i, off):
            cand = off | lax.shift_left(jnp.int32(1), n_bits - 1 - i)
            in_span = (cand ^ INT_MIN) <= (span ^ INT_MIN)
            cand_b = bcast((lo_u + cand) ^ INT_MIN)
            c = count(lambda blk, _: blk >= cand_b)
            return jnp.where(in_span, jnp.where(c >= topk, cand, off), off)

        t = lo_u + lax.fori_loop(0, n_bits, offset_bit, jnp.zeros((Q_BLOCK, 1), I32))
        thr = jnp.maximum(t ^ INT_MIN, KEY_NEG_INF + 1)
        thr_b = bcast(thr)
        thr_ref[:, 0:1] = thr
        n_gt = count(lambda blk, _: blk > thr_b)
        n_eq = count(lambda blk, _: blk == thr_b)
        need = topk - n_gt
        thr_ref[:, 1:2] = jnp.full((Q_BLOCK, 1), 2 * seq, I32)

        @pl.when(jnp.max(n_eq - need) > 0)
        def _ties():
            lane = lax.broadcasted_iota(I32, (Q_BLOCK, LANES), 1)
            nbits = (2 * seq - 1).bit_length()

            def lim_bit(i, lim):
                cand = lim | lax.shift_left(jnp.int32(1), nbits - 1 - i)
                cand_b = bcast(cand)
                c = count(lambda blk, k0: jnp.where(blk == thr_b, k0 + lane, 2 * seq) < cand_b)
                return jnp.where(c <= need, cand, lim)

            thr_ref[:, 1:2] = lax.fori_loop(0, nbits, lim_bit, jnp.zeros((Q_BLOCK, 1), I32))

        lim = thr_ref[:, 1:2]

        def mask_tile(kt, carry):
            k0 = tile_start(kt)
            blk = sc_ref[:, pl.ds(k0, w_tile)]
            bound = jnp.where(k0 + lane_iota < lim, thr, thr + 1)
            add = jnp.where(blk >= bound, 0.0, NEG_BIG).astype(F32)
            add = jnp.where(kt < n_tiles, add, NEG_BIG)
            kaug_ref[A_DC:, pl.ds(k0, w_tile)] = add.astype(BF16)
            return carry

        lax.fori_loop(0, 2 * n_pairs, mask_tile, 0)

    chunk = 16
    eye = (lax.broadcasted_iota(I32, (Q_BLOCK, Q_BLOCK), 0)
           == lax.broadcasted_iota(I32, (Q_BLOCK, Q_BLOCK), 1))
    onehot = jnp.where(eye, 1.0, 0.0).astype(BF16)
    for hh in range(A_HEAD_GROUP):
        qa_ref[hh * Q_BLOCK:(hh + 1) * Q_BLOCK, :A_DC] = q_ref[0, hh]
        qa_ref[hh * Q_BLOCK:(hh + 1) * Q_BLOCK, A_DC:] = onehot
    m_ref[...] = jnp.full(m_ref.shape, NEG_BIG, F32)
    acc_ref[...] = jnp.zeros_like(acc_ref)

    def stage_a(kt, slot):
        k0 = tile_start(kt)
        lg_ref[slot] = jnp.dot(qa_ref[...], kaug_ref[:, pl.ds(k0, w_tile)], preferred_element_type=F32)

    def stage_b(kt, slot, near):
        k0 = tile_start(kt)
        l0 = pl.multiple_of(jnp.clip(_a_strip_off() - (q0 - k0), 0, _a_strip_off()), LANES)
        for hh in range(A_HEAD_GROUP):
            for r0 in range(0, Q_BLOCK, chunk):
                rs = pl.ds(hh * Q_BLOCK + r0, chunk)
                x = lg_ref[slot, rs, :]
                if near:
                    x = x + strip_ref[hh, pl.ds(r0, chunk), pl.ds(l0, w_tile)]
                m_old = m_ref[rs, :]
                m_new = jnp.maximum(m_old, jnp.max(x, axis=1, keepdims=True))
                for c in range(cols):
                    cs = slice(c * LANES, (c + 1) * LANES)
                    p_ref[slot, rs, cs] = jnp.exp2(x[:, cs] - m_new).astype(BF16)
                al_ref[slot, rs, :] = jnp.exp2(m_old - m_new)
                m_ref[rs, :] = m_new

    def stage_c(kt, slot):
        k0 = tile_start(kt)
        pv = jnp.dot(p_ref[slot], kva_ref[0, pl.ds(k0, w_tile), :], preferred_element_type=F32)
        al = al_ref[slot]
        acc_ref[:, :A_DC] = acc_ref[:, :A_DC] * al + pv[:, :A_DC]
        acc_ref[:, A_DC:] = acc_ref[:, A_DC:] * al + pv[:, A_DC:]

    stage_a(0, 0)
    stage_a(1, 1)
    stage_b(0, 0, True)

    def pair_body(i, near):
        kt = 2 * i
        stage_a(kt, 0)
        stage_b(kt - 1, 1, near)
        stage_c(kt - 2, 0)
        stage_a(kt + 1, 1)
        stage_b(kt, 0, near)
        stage_c(kt - 1, 1)

    def run_pairs(lo, hi, near):
        def two_pairs(j, carry):
            pair_body(lo + 2 * j, near)
            pair_body(lo + 2 * j + 1, near)
            return carry

        lax.fori_loop(0, (hi - lo) // 2, two_pairs, 0)

        @pl.when((hi - lo) % 2 == 1)
        def _():
            pair_body(hi - 1, near)

    n_far = jnp.maximum(q0 - (REL_MAX_DIST - 1), 0) // w_tile
    first_near = jnp.maximum((n_far + 1) // 2, 1)
    run_pairs(1, first_near, False)
    run_pairs(first_near, n_pairs, True)
    last = 2 * n_pairs - 1
    stage_b(last, 1, True)
    stage_c(last - 1, 0)
    stage_c(last, 1)

    acc = acc_ref[...]
    out = acc[:, :A_DC] * (1.0 / acc[:, A_DC:A_DC + 1])
    for hh in range(A_HEAD_GROUP):
        o_ref[0, :, hh * A_DC:(hh + 1) * A_DC] = out[hh * Q_BLOCK:(hh + 1) * Q_BLOCK].astype(o_ref.dtype)


def _dsa_attention(qi, wi, kit, q, kvt, kva, strips, topk):
    bsz, _, t, _ = q.shape
    assert t % (2 * A_KEY_TILE) == 0
    groups = A_HEADS // A_HEAD_GROUP
    rows = A_HEAD_GROUP * Q_BLOCK
    resident = dict(pipeline_mode=pl.Buffered(1))
    return pl.pallas_call(
        functools.partial(_dsa_kernel, topk=topk, seq=t),
        grid=(bsz, t // Q_BLOCK, groups),
        in_specs=[
            pl.BlockSpec((1, IDX_HEADS, Q_BLOCK, IDX_DIM), lambda b, i, g: (b, 0, i, 0)),
            pl.BlockSpec((1, IDX_HEADS, Q_BLOCK, 1), lambda b, i, g: (b, 0, i, 0)),
            pl.BlockSpec((1, IDX_DIM, t), lambda b, i, g: (b, 0, 0), **resident),
            pl.BlockSpec((1, A_HEAD_GROUP, Q_BLOCK, A_DC), lambda b, i, g: (b, g, i, 0)),
            pl.BlockSpec((1, A_DC, t), lambda b, i, g: (b, 0, 0), **resident),
            pl.BlockSpec((1, t, 2 * A_DC), lambda b, i, g: (b, 0, 0), **resident),
            pl.BlockSpec((A_HEAD_GROUP, Q_BLOCK, _a_strip_len()), lambda b, i, g: (g, 0, 0)),
        ],
        out_specs=pl.BlockSpec((1, Q_BLOCK, A_HEAD_GROUP * A_DC), lambda b, i, g: (b, i, g)),
        out_shape=jax.ShapeDtypeStruct((bsz, t, A_HEADS * A_DC), BF16),
        scratch_shapes=[pltpu.VMEM((Q_BLOCK, t), I32),
                        pltpu.VMEM((Q_BLOCK, LANES), I32),
                        pltpu.VMEM((2 * A_DC, t), BF16),
                        pltpu.VMEM((rows, 2 * A_DC), BF16),
                        pltpu.VMEM((rows, LANES), F32),
                        pltpu.VMEM((rows, 2 * A_DC), F32),
                        pltpu.VMEM((2, rows, A_KEY_TILE), F32),
                        pltpu.VMEM((2, rows, A_KEY_TILE), BF16),
                        pltpu.VMEM((2, rows, LANES), F32)],
        compiler_params=_params(("parallel", "arbitrary", "arbitrary")),
        name="dsa_attention",
    )(qi, wi, kit, q, kvt, kva, strips)


def _dsa_mixer(a_in, cq_g, ckv_g, w_uq, w_iq, strips, bsz, t):
    w_q = jnp.concatenate([w_uq, w_iq], axis=1).astype(BF16)
    ql, qi, kvt, kva, kit, wit = _dsa_prep(a_in.reshape(bsz, t, -1), cq_g, ckv_g, w_q)
    topk = min(TOPK_MAX, t // 4)
    return _dsa_attention(qi, wit[..., None], kit, ql, kvt, kva, strips, topk)


def _split3(x):
    hi = x.astype(BF16)
    r1 = x - hi.astype(F32)
    mid = r1.astype(BF16)
    lo = (r1 - mid.astype(F32)).astype(BF16)
    return hi, mid, lo


def _ssd_kernel(z_ref, xbc_ref, dt_ref, cw_ref, cb_ref, dtb_ref, alog_ref, dsk_ref, ng_ref, y_ref,
                ext_ref, st_ref, yb_ref, *, d_inner):
    c = pl.program_id(1)
    q = SSM_CHUNK
    n_state = SSM_STATE
    p_dim = SSM_HEADDIM
    hpg = d_inner // p_dim // SSM_GROUPS
    hist = 8

    @pl.when(c == 0)
    def _():
        ext_ref[0:hist, :] = jnp.zeros((hist, ext_ref.shape[1]), F32)
        st_ref[...] = jnp.zeros_like(st_ref)

    ext_ref[hist:hist + q, :] = xbc_ref[0]
    conv = cb_ref[...]
    for k in range(SSM_CONV):
        lo = hist - (SSM_CONV - 1) + k
        conv = conv + cw_ref[k:k + 1, :] * ext_ref[lo:lo + q, :]
    ext_ref[0:hist, :] = ext_ref[q:q + hist, :]
    u = conv * _sigmoid(conv)
    gn = SSM_GROUPS * n_state
    xs = u[:, :d_inner]
    bm = u[:, d_inner:d_inner + gn]
    cm = u[:, d_inner + gn:]

    raw = dt_ref[0] + dtb_ref[...]
    dt = jnp.maximum(raw, 0.0) + jnp.log(1.0 + jnp.exp(-jnp.abs(raw)))
    dta = dt * (-jnp.exp(alog_ref[...]))
    ii = lax.broadcasted_iota(I32, (q, q), 0)
    jj = lax.broadcasted_iota(I32, (q, q), 1)
    tril = ii >= jj
    tri = jnp.where(tril, 1.0, 0.0).astype(BF16)
    cum = sum(jnp.dot(tri, part, preferred_element_type=F32) for part in _split3(dta))
    cum_t = cum.T
    dt_t = dt.T
    last_t = cum_t[:, q - 1:q]
    ecum = jnp.exp(cum)
    wst_t = jnp.exp(last_t - cum_t) * dt_t
    elast = jnp.exp(last_t)

    for g in range(SSM_GROUPS):
        bc = bm[:, g * n_state:(g + 1) * n_state]
        cc = cm[:, g * n_state:(g + 1) * n_state]
        bc16 = bc.astype(BF16)
        cc16 = cc.astype(BF16)
        cb = lax.dot_general(cc16, bc16, (((1,), (1,)), ((), ())), preferred_element_type=F32)
        xs_t = xs[:, g * hpg * p_dim:(g + 1) * hpg * p_dim].T
        for hh in range(hpg):
            h = g * hpg + hh
            seg = cum[:, h:h + 1] - cum_t[h:h + 1, :]
            decay = jnp.exp(jnp.where(tril, seg, -jnp.inf))
            wgt = (decay * cb * dt_t[h:h + 1, :]).astype(BF16)
            xc16 = xs[:, h * p_dim:(h + 1) * p_dim].astype(BF16)
            st = st_ref[h]
            y_off = lax.dot_general(cc16, st.astype(BF16), (((1,), (1,)), ((), ())),
                                    preferred_element_type=F32)
            y = jnp.dot(wgt, xc16, preferred_element_type=F32) + y_off * ecum[:, h:h + 1]
            wx_t = (xs_t[hh * p_dim:(hh + 1) * p_dim, :] * wst_t[h:h + 1, :]).astype(BF16)
            st_ref[h] = st * elast[h:h + 1, :] + jnp.dot(wx_t, bc16, preferred_element_type=F32)
            yb_ref[:, h * p_dim:(h + 1) * p_dim] = y

    z = z_ref[0]
    y = (yb_ref[...] + xs * dsk_ref[...]) * (z * _sigmoid(z))
    gw = hpg * p_dim
    for g in range(SSM_GROUPS):
        yg = y[:, g * gw:(g + 1) * gw]
        yn = yg * lax.rsqrt(jnp.mean(yg * yg, axis=-1, keepdims=True) + EPS)
        y_ref[0, :, g * gw:(g + 1) * gw] = (yn * ng_ref[:, g * gw:(g + 1) * gw]).astype(y_ref.dtype)


def _ssd_mixer(z, xbc, dt, conv_w, conv_b, dt_bias, a_log, d_skip, norm_g):
    bsz, t, d_inner = z.shape
    conv_dim = xbc.shape[-1]
    heads = d_inner // SSM_HEADDIM
    pad = LANES - heads
    dtb = jnp.pad(dt_bias, (0, pad)).reshape(1, LANES)
    alog = jnp.pad(a_log, (0, pad)).reshape(1, LANES)
    dsk = jnp.repeat(d_skip, SSM_HEADDIM).reshape(1, d_inner)
    fixed = lambda b, c: (0, 0)
    return pl.pallas_call(
        functools.partial(_ssd_kernel, d_inner=d_inner),
        grid=(bsz, t // SSM_CHUNK),
        in_specs=[pl.BlockSpec((1, SSM_CHUNK, d_inner), lambda b, c: (b, c, 0)),
                  pl.BlockSpec((1, SSM_CHUNK, conv_dim), lambda b, c: (b, c, 0)),
                  pl.BlockSpec((1, SSM_CHUNK, LANES), lambda b, c: (b, c, 0)),
                  pl.BlockSpec((SSM_CONV, conv_dim), fixed),
                  pl.BlockSpec((1, conv_dim), fixed),
                  pl.BlockSpec((1, LANES), fixed),
                  pl.BlockSpec((1, LANES), fixed),
                  pl.BlockSpec((1, d_inner), fixed),
                  pl.BlockSpec((1, d_inner), fixed)],
        out_specs=pl.BlockSpec((1, SSM_CHUNK, d_inner), lambda b, c: (b, c, 0)),
        out_shape=jax.ShapeDtypeStruct((bsz, t, d_inner), BF16),
        scratch_shapes=[pltpu.VMEM((SSM_CHUNK + 8, conv_dim), F32),
                        pltpu.VMEM((heads, SSM_HEADDIM, SSM_STATE), F32),
                        pltpu.VMEM((SSM_CHUNK, d_inner), F32)],
        compiler_params=_params(("parallel", "arbitrary")),
        name="ssd",
    )(z, xbc, dt, conv_w.reshape(SSM_CONV, conv_dim), conv_b.reshape(1, conv_dim), dtb, alog, dsk,
      norm_g.reshape(1, d_inner))


def _dilated_kernel(q_ref, kc_ref, kp_ref, vc_ref, vp_ref, b_ref, o_ref, l_ref):
    d = C_HEADDIM
    q = q_ref[0]
    k = jnp.concatenate([kp_ref[0], kc_ref[0]], axis=0)
    v = jnp.concatenate([vp_ref[0], vc_ref[0]], axis=0)
    for h in range(C_HEADS):
        sl = slice(h * d, (h + 1) * d)
        lg = lax.dot_general(q[:, sl], k[:, sl], (((1,), (1,)), ((), ())), preferred_element_type=F32)
        x = lg * d ** -0.5 + b_ref[0, h]
        m = jnp.max(x, axis=1, keepdims=True)
        p = jnp.exp(x - m)
        s = jnp.sum(p, axis=1, keepdims=True)
        o = jnp.dot(p.astype(BF16), v[:, sl], preferred_element_type=F32)
        o_ref[0, :, sl] = o * (1.0 / s)
        l_ref[0, :, sl] = jnp.broadcast_to(m + jnp.log(s), (Q_BLOCK, d))


def _dilated_group(qkv, bias, g, dilation):
    bsz, t, cols = qkv.shape
    hd = C_HEADS * C_HEADDIM
    per_tok = cols // hd
    tr = t // dilation
    view = qkv.reshape(bsz, tr, dilation * cols)
    col = lambda c, j: c * per_tok + g * 3 + j
    cur = lambda j: (lambda b, c, n: (b, n, col(c, j)))
    prev = lambda j: (lambda b, c, n: (b, jnp.maximum(n - 1, 0), col(c, j)))
    blk = (1, Q_BLOCK, hd)
    out_spec = pl.BlockSpec(blk, lambda b, c, n: (b, n, c))
    o, lse = pl.pallas_call(
        _dilated_kernel,
        grid=(bsz, dilation, tr // Q_BLOCK),
        in_specs=[pl.BlockSpec(blk, cur(0)), pl.BlockSpec(blk, cur(1)), pl.BlockSpec(blk, prev(1)),
                  pl.BlockSpec(blk, cur(2)), pl.BlockSpec(blk, prev(2)),
                  pl.BlockSpec((1, C_HEADS, Q_BLOCK, 2 * Q_BLOCK),
                               lambda b, c, n: (jnp.minimum(n, 1), 0, 0, 0))],
        out_specs=[out_spec, out_spec],
        out_shape=[jax.ShapeDtypeStruct((bsz, tr, dilation * hd), F32)] * 2,
        compiler_params=_params(("parallel", "parallel", "arbitrary")),
        name="dilated_attention",
    )(view, view, view, view, view, bias)
    return o.reshape(bsz * t, hd), lse.reshape(bsz * t, hd)


def _merge_kernel(x_ref, ya_ref, yb_ref, o1_ref, o2_ref, o3_ref, l1_ref, l2_ref, l3_ref,
                  ga_ref, gb_ref, gc_ref, wa_ref, wb_ref, wc_ref, wo_ref, out_ref):
    l1, l2, l3 = l1_ref[...], l2_ref[...], l3_ref[...]
    lm = jnp.maximum(jnp.maximum(l1, l2), l3)
    e1, e2, e3 = jnp.exp(l1 - lm), jnp.exp(l2 - lm), jnp.exp(l3 - lm)
    yc = (e1 * o1_ref[...] + e2 * o2_ref[...] + e3 * o3_ref[...]) * (1.0 / (e1 + e2 + e3))
    pa = jnp.dot(ya_ref[...], wa_ref[...], preferred_element_type=F32)
    pb = jnp.dot(yb_ref[...], wb_ref[...], preferred_element_type=F32)
    pc = jnp.dot(yc.astype(BF16), wc_ref[...], preferred_element_type=F32)
    merged = _sigmoid(ga_ref[...]) * pa + _sigmoid(gb_ref[...]) * pb + _sigmoid(gc_ref[...]) * pc
    out_ref[...] = x_ref[...] + jnp.dot(merged.astype(BF16), wo_ref[...], preferred_element_type=F32)


def _merge(x, ya, yb, oc, lc, gates, w_a, w_b, w_c, w_o, tm=256):
    m, d = x.shape
    row = lambda i: (i, 0)
    fixed = lambda i: (0, 0)
    full = lambda a: pl.BlockSpec(a.shape, fixed)
    rows = lambda a: pl.BlockSpec((tm, a.shape[1]), row)
    weights = [w_a.astype(BF16), w_b.astype(BF16), w_c.astype(BF16), w_o.astype(BF16)]
    return pl.pallas_call(
        _merge_kernel,
        grid=(m // tm,),
        in_specs=[rows(x), rows(ya), rows(yb)] + [rows(a) for a in oc] + [rows(a) for a in lc]
                 + [pl.BlockSpec((tm, d), lambda i, j=j: (i, j)) for j in range(3)]
                 + [full(w) for w in weights],
        out_specs=pl.BlockSpec((tm, d), row),
        out_shape=jax.ShapeDtypeStruct((m, d), F32),
        compiler_params=_params(("parallel",)),
        name="merge",
    )(x, ya, yb, *oc, *lc, gates, gates, gates, *weights)


def _pad_cols(w, n):
    return jnp.pad(w, ((0, 0), (0, n - w.shape[1])))


def kernel(x, p, rel_bias, final_norm, ffn1_norm, ffn1_gate, ffn1_up, ffn1_down, mix_norm, w_in,
           cq_norm, ckv_norm, w_uq, w_iq, conv_w, conv_b, dt_bias, a_log, d_skip, ssm_norm,
           w_out_a, w_out_b, w_out_c, w_o, ffn2_norm, ffn2_gate, ffn2_up, ffn2_down, ple_norm,
           w_ple_gate, w_ple_proj):
    bsz, t, d_model = x.shape
    depth = w_in.shape[0]
    m = bsz * t
    d_inner = ssm_norm.shape[1]
    conv_dim = conv_b.shape[1]
    ssm_heads = dt_bias.shape[1]
    c_cols = len(C_GROUPS) * 3 * C_HEADS * C_HEADDIM
    sizes = (A_DCQ, A_DC, IDX_DIM, IDX_HEADS, d_inner, conv_dim, ssm_heads, c_cols,
             d_model, d_model, d_model)
    offs = [0]
    for s in sizes:
        offs.append(offs[-1] + s)

    strips = _a_strips(rel_bias)
    c_bias = [_c_bias(rel_bias, dil, A_HEADS + g * C_HEADS) for g, (_, dil) in enumerate(C_GROUPS)]

    x = x.reshape(m, d_model)
    for i in range(depth):
        x = _ffn(x, ffn1_norm[i], ffn1_gate[i], ffn1_up[i], ffn1_down[i])

        w = w_in[i].astype(BF16)
        g_mix = mix_norm[i]
        a_in = _norm_matmul(x, g_mix, _pad_cols(w[:, offs[0]:offs[4]], 4 * LANES), F32, 512, 4 * LANES)
        z = _norm_matmul(x, g_mix, w[:, offs[4]:offs[5]], F32, 1024, 1024)
        xbc = _norm_matmul(x, g_mix, w[:, offs[5]:offs[6]], F32, 1024, 1024)
        dt = _norm_matmul(x, g_mix, _pad_cols(w[:, offs[6]:offs[7]], LANES), F32, 512, LANES)
        qkv = _norm_matmul(x, g_mix, w[:, offs[7]:offs[8]], BF16, 1024, 1536)
        gates = _norm_matmul(x, g_mix, w[:, offs[8]:offs[11]], F32, 1024, 1024)

        ya = _dsa_mixer(a_in, cq_norm[i], ckv_norm[i], w_uq[i], w_iq[i], strips, bsz, t)
        yb = _ssd_mixer(z.reshape(bsz, t, d_inner), xbc.reshape(bsz, t, conv_dim),
                        dt.reshape(bsz, t, LANES), conv_w[i], conv_b[i], dt_bias[i], a_log[i],
                        d_skip[i], ssm_norm[i])
        qkv = qkv.reshape(bsz, t, c_cols)
        oc, lc = zip(*[_dilated_group(qkv, c_bias[g], g, dil) for g, (_, dil) in enumerate(C_GROUPS)])
        x = _merge(x, ya.reshape(m, -1), yb.reshape(m, -1), oc, lc, gates,
                   w_out_a[i], w_out_b[i], w_out_c[i], w_o[i])

        x = _ffn(x, ffn2_norm[i], ffn2_gate[i], ffn2_up[i], ffn2_down[i])
        x = _ple(x, p[i].reshape(m, -1), ple_norm[i], w_ple_gate[i], w_ple_proj[i], final_norm,
                 final_norm=(i == depth - 1))
    return x.reshape(bsz, t, d_model)
```
